```python
import math
import jax, jax.numpy as jnp
from jax import lax
import numpy as np

D_MODEL = 1024
BATCH = 2
SEQ = 16384
DEPTH = 1
DEC_BATCH = 16
DEC_SEQ = 64
PAST_LEN = 1024

CHUNK = 64
HEAD_DIM = 128
N_HEADS = D_MODEL // HEAD_DIM
N_KV_HEADS = 2
ROPE_DIM = HEAD_DIM // 4
ROPE_THETA = 500000.0
N_IDX_HEADS = 8
IDX_DIM = 64
IDX_ROPE_DIM = IDX_DIM // 4
IDX_SCALE = (IDX_DIM * N_IDX_HEADS) ** -0.5
TOPK_KEYS = 256
Q_BLOCK = 128
D_RNN = D_MODEL
N_RNN_BLOCKS = 16
RNN_BLOCK = D_RNN // N_RNN_BLOCKS
CONV_W = 4
LRU_C = 8.0
N_EXPERTS = 32
TOP_K = 4
D_FF = D_MODEL
SWIGLU_LIMIT = 7.0
SWIGLU_ALPHA = 1.702
MOE_BLOCK = 128
D_PLE = 256
DN_ALPHA = (2 * DEPTH) ** 0.25
DN_BETA = (8 * DEPTH) ** -0.25
LN_EPS = 1e-5

IN_SIZES = (N_HEADS * HEAD_DIM, N_KV_HEADS * HEAD_DIM, N_KV_HEADS * HEAD_DIM,
            N_IDX_HEADS * IDX_DIM, IDX_DIM, N_IDX_HEADS, D_RNN, D_RNN, D_MODEL, D_MODEL)
IN_SPLITS = tuple(int(s) for s in np.cumsum(IN_SIZES)[:-1])
D_IN = int(sum(IN_SIZES))

kernel_name = "hybrid_dsa_rglru_moe_stream_step"

f32 = jnp.float32


def layer_norm(x, g, b):
    xf = x.astype(f32)
    mu = jnp.mean(xf, axis=-1, keepdims=True)
    var = jnp.mean(jnp.square(xf - mu), axis=-1, keepdims=True)
    return ((xf - mu) * lax.rsqrt(var + LN_EPS) * g + b).astype(x.dtype)


def rope(x, pos, rot_dim):
    half = rot_dim // 2
    inv = ROPE_THETA ** (-jnp.arange(half, dtype=f32) / half)
    ang = pos.astype(f32)[:, None] * inv[None, :]
    cos = jnp.cos(ang)[:, None, :]
    sin = jnp.sin(ang)[:, None, :]
    xr = x[..., :rot_dim].astype(f32)
    x1, x2 = xr[..., :half], xr[..., half:]
    rot = jnp.concatenate([x1 * cos - x2 * sin, x2 * cos + x1 * sin], axis=-1).astype(x.dtype)
    return jnp.concatenate([rot, x[..., rot_dim:]], axis=-1)


def dsa_block(q, qi, wi, qpos, k, v, ki, n_sel):
    B, Tq = q.shape[:2]
    S = k.shape[1]
    vis_end = jnp.minimum((qpos // CHUNK + 1) * CHUNK, S)
    admissible = jnp.arange(S)[None, :] < vis_end[:, None]
    dots = jnp.einsum('bqhd,bsd->bqhs', qi, ki, preferred_element_type=f32)
    score = jnp.einsum('bqh,bqhs->bqs', wi.astype(f32), jax.nn.relu(dots)) * IDX_SCALE
    score = jnp.where(admissible[None], score, -jnp.inf)
    _, sel = lax.top_k(score, n_sel)
    valid = sel < vis_end[None, :, None]
    k_sel = jax.vmap(lambda kb, sb: kb[sb])(k, sel)
    v_sel = jax.vmap(lambda vb, sb: vb[sb])(v, sel)
    qg = q.reshape(B, Tq, N_KV_HEADS, N_HEADS // N_KV_HEADS, HEAD_DIM)
    s = jnp.einsum('bqkgd,bqnkd->bqkgn', qg, k_sel, preferred_element_type=f32) * (HEAD_DIM ** -0.5)
    s = jnp.where(valid[:, :, None, None, :], s, -jnp.inf)
    p = jax.nn.softmax(s, axis=-1)
    o = jnp.einsum('bqkgn,bqnkd->bqkgd', p.astype(v.dtype), v_sel)
    return o.reshape(B, Tq, N_HEADS * HEAD_DIM)


def dsa_attention(q, qi, wi, qpos, k, v, ki):
    B, T = q.shape[:2]
    n_sel = min(TOPK_KEYS, k.shape[1] // 4)
    if T <= Q_BLOCK:
        return dsa_block(q, qi, wi, qpos, k, v, ki, n_sel)
    nqb = T // Q_BLOCK

    def blocks(a):
        return jnp.moveaxis(a.reshape(B, nqb, Q_BLOCK, *a.shape[2:]), 1, 0)

    out = lax.map(lambda args: dsa_block(*args, k, v, ki, n_sel),
                  (blocks(q), blocks(qi), blocks(wi), qpos.reshape(nqb, Q_BLOCK)))
    return jnp.moveaxis(out, 0, 1).reshape(B, T, N_HEADS * HEAD_DIM)


def causal_conv(xr, conv_state, w_conv, b_conv):
    T = xr.shape[1]
    xp = jnp.concatenate([conv_state.astype(xr.dtype), xr], axis=1)
    out = b_conv
    for j in range(CONV_W):
        out = out + xp[:, j:j + T] * w_conv[j]
    return out, xp[:, -(CONV_W - 1):]


def _lin_comb(left, right):
    a_l, b_l = left
    a_r, b_r = right
    return a_l * a_r, a_r * b_l + b_r


def rg_lru(x, h0, pos, w_a, b_a, w_x, b_x, lam):
    B, T, C = x.shape
    xf = x.astype(f32)
    xb = xf.reshape(B, T, N_RNN_BLOCKS, RNN_BLOCK)
    r = jax.nn.sigmoid(jnp.einsum('btnc,ncd->btnd', xb, w_a.astype(f32)).reshape(B, T, C) + b_a)
    i = jax.nn.sigmoid(jnp.einsum('btnc,ncd->btnd', xb, w_x.astype(f32)).reshape(B, T, C) + b_x)
    log_a = -LRU_C * r * jax.nn.softplus(-lam.astype(f32))
    a = jnp.exp(log_a)
    mult = jnp.sqrt(-jnp.expm1(2.0 * log_a))
    mult = jnp.where((pos == 0)[None, :, None], 1.0, mult)
    b = mult * (i * xf)
    b = b.at[:, 0].add(a[:, 0] * h0.astype(f32))
    _, h = lax.associative_scan(_lin_comb, (a, b), axis=1)
    return h, h[:, -1]


def moe(x, w_router, b_router, w_gate, b_gate, w_up, b_up, w_down, b_down):
    N, D = x.shape
    logits = jnp.matmul(x, w_router).astype(f32) + b_router
    top_val, top_idx = lax.top_k(logits, TOP_K)
    gates = jax.nn.softmax(top_val, axis=-1)
    NK = N * TOP_K
    e_flat = top_idx.reshape(-1)
    tok_flat = jnp.arange(NK, dtype=jnp.int32) // TOP_K
    g_flat = gates.reshape(-1)
    order = jnp.argsort(e_flat)
    e_sorted = e_flat[order]
    counts = jnp.bincount(e_flat, length=N_EXPERTS)
    starts = jnp.cumsum(counts) - counts
    padded = (counts + MOE_BLOCK - 1) // MOE_BLOCK * MOE_BLOCK
    pad_end = jnp.cumsum(padded)
    pad_start = pad_end - padded
    dest = pad_start[e_sorted] + (jnp.arange(NK) - starts[e_sorted])
    n_rows = -(-(NK + N_EXPERTS * (MOE_BLOCK - 1)) // MOE_BLOCK) * MOE_BLOCK
    n_blocks = n_rows // MOE_BLOCK
    row_tok = jnp.full((n_rows,), N, jnp.int32).at[dest].set(tok_flat[order])
    row_gate = jnp.zeros((n_rows,), f32).at[dest].set(g_flat[order])
    block_exp = jnp.minimum(jnp.searchsorted(pad_end, jnp.arange(n_blocks) * MOE_BLOCK, side='right'),
                            N_EXPERTS - 1)
    x_pad = jnp.concatenate([x, jnp.zeros((1, D), x.dtype)], axis=0)
    xb = x_pad[row_tok].reshape(n_blocks, MOE_BLOCK, D)

    def expert_rows(args):
        xe, e = args
        g = jnp.matmul(xe, w_gate[e]) + b_gate[e]
        u = jnp.matmul(xe, w_up[e]) + b_up[e]
        g = jnp.minimum(g, SWIGLU_LIMIT)
        u = jnp.clip(u, -SWIGLU_LIMIT, SWIGLU_LIMIT)
        hmid = g * jax.nn.sigmoid(SWIGLU_ALPHA * g) * (u + 1.0)
        return jnp.matmul(hmid, w_down[e]) + b_down[e]

    yb = lax.map(expert_rows, (xb, block_exp)).reshape(n_rows, D)
    y = jax.ops.segment_sum(yb * row_gate[:, None].astype(yb.dtype), row_tok, num_segments=N + 1)
    return y[:N].astype(x.dtype)


def layer(x, p, pos, past_k, past_v, past_ki, conv_state, h0, params):
    (w_in, w_conv, b_conv, w_a, b_a, w_x, b_x, lam, w_out, ln1_g, ln1_b,
     w_router, b_router, w_gate, b_gate, w_up, b_up, w_down, b_down,
     ln2_g, ln2_b, w_ple_gate, w_ple_proj) = params
    B, T, _ = x.shape
    h = jnp.matmul(x, w_in)
    q, k, v, qi, ki, wi, xr, gr, ga, gb = jnp.split(h, IN_SPLITS, axis=-1)
    q = rope(q.reshape(B, T, N_HEADS, HEAD_DIM), pos, ROPE_DIM)
    k = rope(k.reshape(B, T, N_KV_HEADS, HEAD_DIM), pos, ROPE_DIM)
    v = v.reshape(B, T, N_KV_HEADS, HEAD_DIM)
    qi = rope(qi.reshape(B, T, N_IDX_HEADS, IDX_DIM), pos, IDX_ROPE_DIM)
    ki = rope(ki[:, :, None, :], pos, IDX_ROPE_DIM)[:, :, 0]
    k_all = jnp.concatenate([past_k.astype(k.dtype), k], axis=1)
    v_all = jnp.concatenate([past_v.astype(v.dtype), v], axis=1)
    ki_all = jnp.concatenate([past_ki.astype(ki.dtype), ki], axis=1)
    attn = dsa_attention(q, qi, wi, pos, k_all, v_all, ki_all)
    conv_out, new_conv = causal_conv(xr, conv_state, w_conv, b_conv)
    hr, h_last = rg_lru(conv_out, h0, pos, w_a, b_a, w_x, b_x, lam)
    rnn = hr.astype(x.dtype) * jax.nn.gelu(gr)
    merged = jax.nn.sigmoid(ga) * attn + jax.nn.sigmoid(gb) * rnn
    x1 = layer_norm(DN_ALPHA * x + jnp.matmul(merged, w_out), ln1_g, ln1_b)
    ffn = moe(x1.reshape(B * T, D_MODEL), w_router, b_router, w_gate, b_gate,
              w_up, b_up, w_down, b_down).reshape(B, T, D_MODEL)
    x2 = layer_norm(DN_ALPHA * x1 + ffn, ln2_g, ln2_b)
    y = x2 + jax.nn.sigmoid(jnp.matmul(x2, w_ple_gate)) * jnp.matmul(p, w_ple_proj)
    return y, k, v, ki, new_conv, h_last.astype(x.dtype)


def setup_inputs(seed: int = 0) -> dict:
    key = jax.random.key(seed)
    ks = iter(jax.random.split(key, 48))

    def nrm(shape, scale):
        return scale * jax.random.normal(next(ks), shape, jnp.float32)

    u = jax.random.uniform(next(ks), (DEPTH, D_RNN), jnp.float32, 0.9, 0.999) ** (1.0 / LRU_C)
    lru_lambda = jnp.log(u) - jnp.log1p(-u)
    return {
        "x_prompt": nrm((BATCH, SEQ, D_MODEL), 1.0),
        "x_sample": nrm((DEC_BATCH, DEC_SEQ, D_MODEL), 1.0),
        "cache_k": nrm((DEPTH, DEC_BATCH, PAST_LEN, N_KV_HEADS, HEAD_DIM), 1.0),
        "cache_v": nrm((DEPTH, DEC_BATCH, PAST_LEN, N_KV_HEADS, HEAD_DIM), 1.0),
        "cache_kidx": nrm((DEPTH, DEC_BATCH, PAST_LEN, IDX_DIM), 1.0),
        "state_conv": nrm((DEPTH, DEC_BATCH, CONV_W - 1, D_RNN), 1.0),
        "state_h": nrm((DEPTH, DEC_BATCH, D_RNN), 0.5),
        "p_prompt": nrm((DEPTH, BATCH, SEQ, D_PLE), 1.0),
        "p_sample": nrm((DEPTH, DEC_BATCH, DEC_SEQ, D_PLE), 1.0),
        "w_in": nrm((DEPTH, D_MODEL, D_IN), D_MODEL ** -0.5),
        "w_conv": nrm((DEPTH, CONV_W, D_RNN), CONV_W ** -0.5),
        "b_conv": nrm((DEPTH, D_RNN), 0.02),
        "w_a": nrm((DEPTH, N_RNN_BLOCKS, RNN_BLOCK, RNN_BLOCK), RNN_BLOCK ** -0.5),
        "b_a": nrm((DEPTH, D_RNN), 0.02),
        "w_x": nrm((DEPTH, N_RNN_BLOCKS, RNN_BLOCK, RNN_BLOCK), RNN_BLOCK ** -0.5),
        "b_x": nrm((DEPTH, D_RNN), 0.02),
        "lru_lambda": lru_lambda,
        "w_out": nrm((DEPTH, D_MODEL, D_MODEL), DN_BETA * D_MODEL ** -0.5),
        "ln1_g": 1.0 + nrm((DEPTH, D_MODEL), 0.02),
        "ln1_b": nrm((DEPTH, D_MODEL), 0.02),
        "w_router": nrm((DEPTH, D_MODEL, N_EXPERTS), D_MODEL ** -0.5),
        "b_router": nrm((DEPTH, N_EXPERTS), 0.01),
        "w_gate": nrm((DEPTH, N_EXPERTS, D_MODEL, D_FF), D_MODEL ** -0.5),
        "b_gate": nrm((DEPTH, N_EXPERTS, D_FF), 0.02),
        "w_up": nrm((DEPTH, N_EXPERTS, D_MODEL, D_FF), D_MODEL ** -0.5),
        "b_up": nrm((DEPTH, N_EXPERTS, D_FF), 0.02),
        "w_down": nrm((DEPTH, N_EXPERTS, D_FF, D_MODEL), DN_BETA * D_FF ** -0.5),
        "b_down": nrm((DEPTH, N_EXPERTS, D_MODEL), 0.02),
        "ln2_g": 1.0 + nrm((DEPTH, D_MODEL), 0.02),
        "ln2_b": nrm((DEPTH, D_MODEL), 0.02),
        "w_ple_gate": nrm((DEPTH, D_MODEL, D_MODEL), D_MODEL ** -0.5),
        "w_ple_proj": nrm((DEPTH, D_PLE, D_MODEL), D_PLE ** -0.5),
    }


def reference(x_prompt, x_sample, cache_k, cache_v, cache_kidx, state_conv, state_h,
              p_prompt, p_sample, w_in, w_conv, b_conv, w_a, b_a, w_x, b_x, lru_lambda,
              w_out, ln1_g, ln1_b, w_router, b_router, w_gate, b_gate, w_up, b_up,
              w_down, b_down, ln2_g, ln2_b, w_ple_gate, w_ple_proj):
    Bp, Tp, _ = x_prompt.shape
    Bs, Ts, _ = x_sample.shape
    past_len = cache_k.shape[2]
    pos_p = jnp.arange(Tp)
    pos_s = past_len + jnp.arange(Ts)
    dt = x_prompt.dtype
    empty_kv = jnp.zeros((Bp, 0, N_KV_HEADS, HEAD_DIM), dt)
    empty_ki = jnp.zeros((Bp, 0, IDX_DIM), dt)
    zero_conv = jnp.zeros((Bp, CONV_W - 1, D_RNN), dt)
    zero_h = jnp.zeros((Bp, D_RNN), dt)

    yp, ys = x_prompt, x_sample
    kp, vp, kip, cp, hp = [], [], [], [], []
    ksm, vsm, kism, csm, hsm = [], [], [], [], []
    for l in range(DEPTH):
        params = (w_in[l], w_conv[l], b_conv[l], w_a[l], b_a[l], w_x[l], b_x[l], lru_lambda[l],
                  w_out[l], ln1_g[l], ln1_b[l], w_router[l], b_router[l], w_gate[l], b_gate[l],
                  w_up[l], b_up[l], w_down[l], b_down[l], ln2_g[l], ln2_b[l],
                  w_ple_gate[l], w_ple_proj[l])
        yp, k1, v1, ki1, c1, h1 = layer(yp, p_prompt[l], pos_p, empty_kv, empty_kv, empty_ki,
                                        zero_conv, zero_h, params)
        ys, k2, v2, ki2, c2, h2 = layer(ys, p_sample[l], pos_s, cache_k[l], cache_v[l], cache_kidx[l],
                                        state_conv[l], state_h[l], params)
        kp.append(k1); vp.append(v1); kip.append(ki1); cp.append(c1); hp.append(h1)
        ksm.append(k2); vsm.append(v2); kism.append(ki2); csm.append(c2); hsm.append(h2)

    return (yp, ys,
            jnp.stack(kp), jnp.stack(vp), jnp.stack(kip), jnp.stack(cp), jnp.stack(hp),
            jnp.stack(ksm), jnp.stack(vsm), jnp.stack(kism), jnp.stack(csm), jnp.stack(hsm))
```

```python
import functools

import jax
import jax.numpy as jnp
import numpy as np
from jax import lax
from jax.experimental import pallas as pl
from jax.experimental.pallas import tpu as pltpu

f32 = jnp.float32
i32 = jnp.int32
MXU_DTYPE = jnp.bfloat16

CHUNK = 64
HEAD_DIM = 128
N_KV_HEADS = 2
GROUP = 4
ROPE_DIM = 32
ROPE_THETA = 500000.0
N_IDX_HEADS = 8
IDX_DIM = 64
IDX_ROPE_DIM = 16
IDX_SCALE = (IDX_DIM * N_IDX_HEADS) ** -0.5
TOPK_KEYS = 256
Q_BLOCK = 128
N_RNN_BLOCKS = 16
RNN_BLOCK = 64
RNN_GROUP = 256
CONV_W = 4
LRU_C = 8.0
N_EXPERTS = 32
TOP_K = 4
SWIGLU_LIMIT = 7.0
SWIGLU_ALPHA = 1.702
LN_EPS = 1e-5

LANES = 128
SUBLANES = 8
VMEM_LIMIT = 56 * 1024 * 1024
INT_MIN = -2 ** 31
NEG_INF = float("-inf")


def _cparams(sem):
    return pltpu.CompilerParams(dimension_semantics=sem, vmem_limit_bytes=VMEM_LIMIT)


def _resident(shape, index_map):
    return pl.BlockSpec(shape, index_map, pipeline_mode=pl.Buffered(1))


def _dot(a, b):
    return jnp.dot(a, b, preferred_element_type=f32)


_C_Q, _C_K, _C_V, _C_QI, _C_KW, _C_XR, _C_GR, _C_GA, _C_GB, _C_END = (
    0, 1024, 1280, 1536, 2048, 2176, 3200, 4224, 5248, 6272)


def _rope_tile(h, c, s_lo, s_hi, half):
    return h * c + pltpu.roll(h, half, 1) * s_hi + pltpu.roll(h, LANES - half, 1) * s_lo


def _inproj_kernel(x_ref, w_ref, ca_ref, sla_ref, sha_ref, ci_ref, sli_ref, shi_ref,
                   q_ref, k_ref, kb_ref, v_ref, vb_ref, qi_ref, ki_ref, kib_ref, wi_ref,
                   xr_ref, gr_ref, ga_ref, gb_ref):
    xb = x_ref[...].astype(MXU_DTYPE)

    def proj(a, b):
        return _dot(xb, w_ref[:, a:b])

    ca, sla, sha = ca_ref[...], sla_ref[...], sha_ref[...]
    ci, sli, shi = ci_ref[...], sli_ref[...], shi_ref[...]

    hq = proj(_C_Q, _C_K)
    for j in range(hq.shape[1] // LANES):
        t = _rope_tile(hq[:, j * LANES:(j + 1) * LANES], ca, sla, sha, ROPE_DIM // 2)
        q_ref[:, j * LANES:(j + 1) * LANES] = t.astype(q_ref.dtype)
    hk = proj(_C_K, _C_V)
    for j in range(hk.shape[1] // LANES):
        t = _rope_tile(hk[:, j * LANES:(j + 1) * LANES], ca, sla, sha, ROPE_DIM // 2)
        k_ref[:, j * LANES:(j + 1) * LANES] = t
        kb_ref[:, j * LANES:(j + 1) * LANES] = t.astype(kb_ref.dtype)
    hv = proj(_C_V, _C_QI)
    v_ref[...] = hv
    vb_ref[...] = hv.astype(vb_ref.dtype)
    hqi = proj(_C_QI, _C_KW)
    for j in range(hqi.shape[1] // LANES):
        t = _rope_tile(hqi[:, j * LANES:(j + 1) * LANES], ci, sli, shi, IDX_ROPE_DIM // 2)
        qi_ref[:, j * LANES:(j + 1) * LANES] = t.astype(qi_ref.dtype)
    hkw = proj(_C_KW, _C_XR)
    t = _rope_tile(hkw, ci, sli, shi, IDX_ROPE_DIM // 2)
    ki_ref[...] = t[:, :IDX_DIM]
    kib_ref[...] = t[:, :IDX_DIM].astype(kib_ref.dtype)
    wi_ref[...] = hkw[:, IDX_DIM:IDX_DIM + N_IDX_HEADS]
    xr_ref[...] = proj(_C_XR, _C_GR)
    gr_ref[...] = proj(_C_GR, _C_GA)
    ga_ref[...] = proj(_C_GA, _C_GB)
    gb_ref[...] = proj(_C_GB, _C_END)


def _rope_tables(pos, rot_dim, head_dim):
    half = rot_dim // 2
    inv = ROPE_THETA ** (-jnp.arange(half, dtype=f32) / half)
    ang = pos.astype(f32)[:, None] * inv[None, :]
    cos, sin = jnp.cos(ang), jnp.sin(ang)
    n = pos.shape[0]
    one = jnp.ones((n, head_dim - rot_dim), f32)
    zero_h = jnp.zeros((n, half), f32)
    zero_r = jnp.zeros((n, head_dim - rot_dim), f32)
    c = jnp.concatenate([cos, cos, one], axis=1)
    s_lo = jnp.concatenate([-sin, zero_h, zero_r], axis=1)
    s_hi = jnp.concatenate([zero_h, sin, zero_r], axis=1)
    rep = LANES // head_dim
    return tuple(jnp.tile(t, (1, rep)) for t in (c, s_lo, s_hi))


def _pack_w_in(w_in):
    d = w_in.shape[0]
    pad = jnp.zeros((d, LANES - IDX_DIM - N_IDX_HEADS), w_in.dtype)
    split = 1024 + 256 + 256 + 512 + IDX_DIM + N_IDX_HEADS
    return jnp.concatenate([w_in[:, :split], pad, w_in[:, split:]], axis=1).astype(MXU_DTYPE)


def _in_projection(x2d, w_packed, tabs_a, tabs_i, tm):
    n, d = x2d.shape
    nt = tabs_a[0].shape[0] // tm
    row = lambda i: (i, 0)
    tab = lambda i: (i % nt, 0)
    out_cols = [(1024, MXU_DTYPE), (256, f32), (256, MXU_DTYPE), (256, f32), (256, MXU_DTYPE),
                (512, MXU_DTYPE), (IDX_DIM, f32), (IDX_DIM, MXU_DTYPE), (N_IDX_HEADS, f32),
                (1024, f32), (1024, f32), (1024, f32), (1024, f32)]
    return pl.pallas_call(
        _inproj_kernel,
        grid=(n // tm,),
        in_specs=[pl.BlockSpec((tm, d), row), _resident(w_packed.shape, lambda i: (0, 0))]
                 + [pl.BlockSpec((tm, LANES), tab)] * 6,
        out_specs=[pl.BlockSpec((tm, c), row) for c, _ in out_cols],
        out_shape=[jax.ShapeDtypeStruct((n, c), dt) for c, dt in out_cols],
        compiler_params=_cparams(("parallel",)),
        name="in_projection",
    )(x2d, w_packed, *tabs_a, *tabs_i)


def _dsa_kernel(q_ref, qi_ref, wi_ref, kit_ref, kt_ref, v_ref, o_ref,
                keys_scr, m_scr, l_scr, acc_scr, j_scr,
                *, qb, kb, s_true, s_pad, pos0, n_sel):
    i = pl.program_id(1)
    row = lax.broadcasted_iota(i32, (qb, 1), 0)
    pos = pos0 + i * qb + row
    vis_end = jnp.minimum((pos // CHUNK + 1) * CHUNK, s_true)
    pos_last = pos0 + i * qb + (qb - 1)
    kend = jnp.minimum((pos_last // CHUNK + 1) * CHUNK, s_true)
    nkb = (kend + kb - 1) // kb
    lane = lax.broadcasted_iota(i32, (qb, kb), 1)

    qi = qi_ref[...]
    wi = wi_ref[...]

    def score_block(b, carry):
        kit = kit_ref[b]
        sc = jnp.zeros((qb, kb), f32)
        for h in range(N_IDX_HEADS):
            d = _dot(qi[:, h * IDX_DIM:(h + 1) * IDX_DIM], kit)
            sc = sc + wi[:, h:h + 1] * jnp.maximum(d, 0.0)
        sc = sc * IDX_SCALE
        sc = jnp.where(sc == 0.0, 0.0, sc)
        bits = pltpu.bitcast(sc, i32)
        key = bits ^ ((bits >> 31) & 0x7FFFFFFF)
        key = jnp.where(b * kb + lane < vis_end, key, INT_MIN)
        keys_scr[b] = key
        return carry

    lax.fori_loop(0, nkb, score_block, 0)

    def count(pred):
        def body(b, acc):
            ind = jnp.where(pred(keys_scr[b], b), 1.0, 0.0)
            part = ind[:, 0:LANES]
            for t in range(1, kb // LANES):
                part = part + ind[:, t * LANES:(t + 1) * LANES]
            return acc + part
        acc = lax.fori_loop(0, nkb, body, jnp.zeros((qb, LANES), f32))
        return jnp.sum(acc, axis=1, keepdims=True)

    nsel = jnp.float32(n_sel)
    c0 = count(lambda k, b: k >= 0)
    tau0 = jnp.where(c0 >= nsel, 0, INT_MIN).astype(i32)

    def bit_step(t, tau):
        cand = tau + jnp.left_shift(jnp.int32(1), 30 - t)
        c = count(lambda k, b: k >= cand)
        return jnp.where(c >= nsel, cand, tau)

    tau = lax.fori_loop(0, 31, bit_step, tau0)

    need = nsel - count(lambda k, b: k > tau)
    c_eq = count(lambda k, b: k == tau)
    j_scr[...] = jnp.full((qb, 1), s_pad, i32)

    @pl.when(jnp.max(jnp.where(tau == INT_MIN, 0.0, c_eq - need)) > 0.0)
    def _():
        def jbit(t, jlo):
            cand = jlo + jnp.left_shift(jnp.int32(1), (s_pad - 1).bit_length() - 1 - t)
            c = count(lambda k, b: (k == tau) & (b * kb + lane < cand))
            return jnp.where(c < need, cand, jlo)
        jlo = lax.fori_loop(0, (s_pad - 1).bit_length(), jbit, jnp.zeros((qb, 1), i32))
        j_scr[...] = jlo + 1

    jcut = jnp.where(tau == INT_MIN, 0, j_scr[...])

    q = q_ref[...]
    qg = [jnp.concatenate([q[:, (g * GROUP + j) * HEAD_DIM:(g * GROUP + j + 1) * HEAD_DIM]
                           for j in range(GROUP)], axis=0) for g in range(N_KV_HEADS)]
    m_scr[...] = jnp.full(m_scr.shape, NEG_INF, f32)
    l_scr[...] = jnp.zeros(l_scr.shape, f32)
    acc_scr[...] = jnp.zeros(acc_scr.shape, f32)
    scale = HEAD_DIM ** -0.5

    def attn_block(b, carry):
        k = keys_scr[b]
        selected = (k > tau) | ((k == tau) & (b * kb + lane < jcut))
        bias = jnp.where(selected, 0.0, NEG_INF)
        for g in range(N_KV_HEADS):
            s = _dot(qg[g], kt_ref[b, g * HEAD_DIM:(g + 1) * HEAD_DIM, :]) * scale
            s = (s.reshape(GROUP, qb, kb) + bias[None]).reshape(GROUP * qb, kb)
            m_old = m_scr[g]
            m_new = jnp.maximum(m_old, jnp.max(s, axis=1, keepdims=True))
            m_safe = jnp.where(m_new == NEG_INF, 0.0, m_new)
            p = jnp.exp(s - m_safe)
            alpha = jnp.exp(m_old - m_safe)
            l_scr[g] = alpha * l_scr[g] + jnp.sum(p, axis=1, keepdims=True)
            pv = _dot(p.astype(MXU_DTYPE), v_ref[b, :, g * HEAD_DIM:(g + 1) * HEAD_DIM])
            acc_scr[g] = alpha * acc_scr[g] + pv
            m_scr[g] = m_new
        return carry

    lax.fori_loop(0, nkb, attn_block, 0)

    for g in range(N_KV_HEADS):
        o = acc_scr[g] / l_scr[g]
        for j in range(GROUP):
            h = g * GROUP + j
            o_ref[:, h * HEAD_DIM:(h + 1) * HEAD_DIM] = o[j * qb:(j + 1) * qb, :]


def _dsa_attention(q, qi, wi, ki_all, k_all, v_all, *, qb, kb, s_true, pos0):
    bsz, t, _ = q.shape
    s_pad = k_all.shape[1]
    nk = s_pad // kb
    n_sel = min(TOPK_KEYS, s_true // 4)
    kit = ki_all.reshape(bsz, nk, kb, IDX_DIM).transpose(0, 1, 3, 2)
    kt = k_all.reshape(bsz, nk, kb, N_KV_HEADS * HEAD_DIM).transpose(0, 1, 3, 2)
    vv = v_all.reshape(bsz, nk, kb, N_KV_HEADS * HEAD_DIM)
    qrow = lambda b, i: (b, i, 0)
    kmap = lambda b, i: (b, 0, 0, 0)
    kern = functools.partial(_dsa_kernel, qb=qb, kb=kb, s_true=s_true, s_pad=s_pad, pos0=pos0, n_sel=n_sel)
    return pl.pallas_call(
        kern,
        grid=(bsz, t // qb),
        in_specs=[pl.BlockSpec((None, qb, q.shape[2]), qrow),
                  pl.BlockSpec((None, qb, qi.shape[2]), qrow),
                  pl.BlockSpec((None, qb, wi.shape[2]), qrow),
                  _resident((None, nk, IDX_DIM, kb), kmap),
                  _resident((None, nk, N_KV_HEADS * HEAD_DIM, kb), kmap),
                  _resident((None, nk, kb, N_KV_HEADS * HEAD_DIM), kmap)],
        out_specs=pl.BlockSpec((None, qb, q.shape[2]), qrow),
        out_shape=jax.ShapeDtypeStruct(q.shape, f32),
        scratch_shapes=[pltpu.VMEM((nk, qb, kb), i32),
                        pltpu.VMEM((N_KV_HEADS, GROUP * qb, 1), f32),
                        pltpu.VMEM((N_KV_HEADS, GROUP * qb, 1), f32),
                        pltpu.VMEM((N_KV_HEADS, GROUP * qb, HEAD_DIM), f32),
                        pltpu.VMEM((qb, 1), i32)],
        compiler_params=_cparams(("parallel", "arbitrary")),
        name="dsa_attention",
    )(q, qi, wi, kit, kt, vv)


def _expm1(y):
    u = jnp.exp(y)
    um1 = u - 1.0
    return jnp.where(um1 == 0.0, y, jnp.where(um1 == -1.0, -1.0, um1 * y / jnp.log(u)))


def _rglru_kernel(xr_ref, gr_ref, cs_ref, h0_ref, wc_ref, bc_ref, wax_ref, ba_ref, bx_ref, lam_ref,
                  rnn_ref, hlast_ref, xp_scr, a_scr, b_scr, h_scr, *, reset_first):
    t = pl.program_id(1)
    tt, c = xr_ref.shape
    lead = SUBLANES

    @pl.when(t == 0)
    def _():
        xp_scr[lead - (CONV_W - 1):lead, :] = cs_ref[...]
        h_scr[...] = h0_ref[...]

    xp_scr[lead:lead + tt, :] = xr_ref[...]
    conv = bc_ref[...]
    for j in range(CONV_W):
        off = lead - (CONV_W - 1) + j
        conv = conv + xp_scr[off:off + tt, :] * wc_ref[j:j + 1, :]
    xp_scr[lead - (CONV_W - 1):lead, :] = xp_scr[lead + tt - (CONV_W - 1):lead + tt, :]

    cb = conv.astype(MXU_DTYPE)
    sp = lam_ref[...]
    grow = t * tt + lax.broadcasted_iota(i32, (tt, 1), 0)
    for g in range(c // RNN_GROUP):
        sl = slice(g * RNN_GROUP, (g + 1) * RNN_GROUP)
        z = _dot(cb[:, sl], wax_ref[g])
        r = jax.nn.sigmoid(z[:, :RNN_GROUP] + ba_ref[:, sl])
        ig = jax.nn.sigmoid(z[:, RNN_GROUP:] + bx_ref[:, sl])
        log_a = -LRU_C * r * sp[:, sl]
        mult = jnp.sqrt(-_expm1(2.0 * log_a))
        if reset_first:
            mult = jnp.where(grow == 0, 1.0, mult)
        a_scr[:, sl] = jnp.exp(log_a)
        b_scr[:, sl] = mult * (ig * conv[:, sl])

    def rows(jb, h):
        base = pl.multiple_of(jb * SUBLANES, SUBLANES)
        for u in range(SUBLANES):
            h = a_scr[pl.ds(base + u, 1), :] * h + b_scr[pl.ds(base + u, 1), :]
            a_scr[pl.ds(base + u, 1), :] = h
        return h

    h = lax.fori_loop(0, tt // SUBLANES, rows, h_scr[...])
    h_scr[...] = h
    rnn_ref[...] = a_scr[...] * jax.nn.gelu(gr_ref[...])

    @pl.when(t == pl.num_programs(1) - 1)
    def _():
        hlast_ref[...] = h


def _blockdiag_groups(w):
    per = RNN_GROUP // RNN_BLOCK
    g = w.reshape(N_RNN_BLOCKS // per, per, RNN_BLOCK, RNN_BLOCK)
    eye = jnp.eye(per, dtype=w.dtype)
    return jnp.einsum('gacd,ab->gacbd', g, eye).reshape(N_RNN_BLOCKS // per, RNN_GROUP, RNN_GROUP)


def _rglru(xr, gr, conv_state, h0, w_conv, b_conv, w_a, b_a, w_x, b_x, lam, *, tt, reset_first):
    bsz, t, c = xr.shape
    wax = jnp.concatenate([_blockdiag_groups(w_a), _blockdiag_groups(w_x)], axis=2).astype(MXU_DTYPE)
    sp = jax.nn.softplus(-lam.astype(f32)).reshape(1, c)
    seq = lambda b, i: (b, i, 0)
    per_b = lambda b, i: (b, 0, 0)
    const2 = lambda b, i: (0, 0)
    const3 = lambda b, i: (0, 0, 0)
    kern = functools.partial(_rglru_kernel, reset_first=reset_first)
    return pl.pallas_call(
        kern,
        grid=(bsz, t // tt),
        in_specs=[pl.BlockSpec((None, tt, c), seq), pl.BlockSpec((None, tt, c), seq),
                  pl.BlockSpec((None, CONV_W - 1, c), per_b), pl.BlockSpec((None, 1, c), per_b),
                  pl.BlockSpec((CONV_W, c), const2), pl.BlockSpec((1, c), const2),
                  pl.BlockSpec(wax.shape, const3), pl.BlockSpec((1, c), const2),
                  pl.BlockSpec((1, c), const2), pl.BlockSpec((1, c), const2)],
        out_specs=[pl.BlockSpec((None, tt, c), seq), pl.BlockSpec((None, 1, c), per_b)],
        out_shape=[jax.ShapeDtypeStruct((bsz, t, c), f32), jax.ShapeDtypeStruct((bsz, 1, c), f32)],
        scratch_shapes=[pltpu.VMEM((tt + SUBLANES, c), f32), pltpu.VMEM((tt, c), f32),
                        pltpu.VMEM((tt, c), f32), pltpu.VMEM((1, c), f32)],
        compiler_params=_cparams(("parallel", "arbitrary")),
        name="rglru",
    )(xr, gr, conv_state, h0.reshape(bsz, 1, c), w_conv, b_conv.reshape(1, c), wax,
      b_a.reshape(1, c), b_x.reshape(1, c), sp)


def _layer_norm(z, g, b):
    mu = jnp.mean(z, axis=-1, keepdims=True)
    var = jnp.mean(jnp.square(z - mu), axis=-1, keepdims=True)
    return (z - mu) * lax.rsqrt(var + LN_EPS) * g + b


def _merge_kernel(attn_ref, rnn_ref, ga_ref, gb_ref, x_ref, wo_ref, g1_ref, b1_ref, wr_ref, br_ref,
                  x1_ref, x1b_ref, cw_ref, *, dn_alpha):
    merged = jax.nn.sigmoid(ga_ref[...]) * attn_ref[...] + jax.nn.sigmoid(gb_ref[...]) * rnn_ref[...]
    m = _dot(merged.astype(MXU_DTYPE), wo_ref[...])
    x1 = _layer_norm(dn_alpha * x_ref[...] + m, g1_ref[...], b1_ref[...])
    x1_ref[...] = x1
    x1b = x1.astype(MXU_DTYPE)
    x1b_ref[...] = x1b
    logits = _dot(x1b, wr_ref[...]) + br_ref[...]
    ne = logits.shape[1]
    eidx = lax.broadcasted_iota(i32, logits.shape, 1).astype(f32)
    work = logits
    chosen = jnp.zeros(logits.shape, jnp.bool_)
    top = None
    for kk in range(TOP_K):
        mx = jnp.max(work, axis=1, keepdims=True)
        if kk == 0:
            top = mx
        first = jnp.min(jnp.where(work == mx, eidx, ne), axis=1, keepdims=True)
        pick = eidx == first
        chosen = chosen | pick
        work = jnp.where(pick, NEG_INF, work)
    e = jnp.where(chosen, jnp.exp(logits - top), 0.0)
    cw_ref[...] = e / jnp.sum(e, axis=1, keepdims=True)


def _merge_ln_router(attn, rnn, ga, gb, x2d, w_out, ln_g, ln_b, w_router, b_router, *, tm, dn_alpha):
    n, d = x2d.shape
    ne = w_router.shape[1]
    row = lambda i: (i, 0)
    const = lambda i: (0, 0)
    kern = functools.partial(_merge_kernel, dn_alpha=dn_alpha)
    return pl.pallas_call(
        kern,
        grid=(n // tm,),
        in_specs=[pl.BlockSpec((tm, d), row)] * 5
                 + [pl.BlockSpec((d, d), const), pl.BlockSpec((1, d), const), pl.BlockSpec((1, d), const),
                    pl.BlockSpec((d, ne), const), pl.BlockSpec((1, ne), const)],
        out_specs=[pl.BlockSpec((tm, d), row), pl.BlockSpec((tm, d), row), pl.BlockSpec((tm, ne), row)],
        out_shape=[jax.ShapeDtypeStruct((n, d), f32), jax.ShapeDtypeStruct((n, d), MXU_DTYPE),
                   jax.ShapeDtypeStruct((n, ne), f32)],
        compiler_params=_cparams(("parallel",)),
        name="merge_ln_router",
    )(attn, rnn, ga, gb, x2d, w_out.astype(MXU_DTYPE), ln_g.reshape(1, d), ln_b.reshape(1, d),
      w_router.astype(MXU_DTYPE), b_router.reshape(1, ne))


MOE_CHUNK = 128


def _moe_kernel(xb_ref, cw_ref, wg_ref, bg_ref, wu_ref, bu_ref, wd_ref, bd_ref, o_ref,
                rank_scr, rankt_scr):
    e = pl.program_id(1)
    tm, ne = cw_ref.shape

    @pl.when(e == 0)
    def _():
        sel = jnp.where(cw_ref[...] > 0.0, 1.0, 0.0).astype(MXU_DTYPE)
        r = lax.broadcasted_iota(i32, (tm, tm), 0)
        c = lax.broadcasted_iota(i32, (tm, tm), 1)
        lower = jnp.where(c < r, 1.0, 0.0).astype(MXU_DTYPE)
        upper = jnp.where(r < c, 1.0, 0.0).astype(MXU_DTYPE)
        rank = _dot(lower, sel)
        rank_scr[...] = jnp.where(sel > 0, rank, -1.0)
        eye = jnp.where(lax.broadcasted_iota(i32, (ne, ne), 0) == lax.broadcasted_iota(i32, (ne, ne), 1),
                        1.0, 0.0).astype(MXU_DTYPE)
        selt = lax.dot_general(eye, sel, (((1,), (1,)), ((), ())), preferred_element_type=f32)
        rankt = _dot(selt.astype(MXU_DTYPE), upper)
        rankt_scr[...] = jnp.where(selt > 0, rankt, -1.0)
        o_ref[...] = jnp.zeros(o_ref.shape, f32)

    rrow = rankt_scr[pl.ds(e, 1), :]
    emask = lax.broadcasted_iota(i32, (tm, ne), 1) == e
    rcol = jnp.sum(jnp.where(emask, rank_scr[...], 0.0), axis=1, keepdims=True)
    gate = jnp.sum(jnp.where(emask, cw_ref[...], 0.0), axis=1, keepdims=True)
    cnt = jnp.max(rrow).astype(i32) + 1
    sub = lax.broadcasted_iota(i32, (MOE_CHUNK, tm), 0).astype(f32)
    lan = lax.broadcasted_iota(i32, (tm, MOE_CHUNK), 1).astype(f32)

    def chunk(cidx, carry):
        lo = (cidx * MOE_CHUNK).astype(f32)
        gather = jnp.where(rrow - lo == sub, 1.0, 0.0).astype(MXU_DTYPE)
        xg = _dot(gather, xb_ref[...]).astype(MXU_DTYPE)
        g = _dot(xg, wg_ref[...]) + bg_ref[...]
        u = _dot(xg, wu_ref[...]) + bu_ref[...]
        g = jnp.minimum(g, SWIGLU_LIMIT)
        u = jnp.clip(u, -SWIGLU_LIMIT, SWIGLU_LIMIT)
        hmid = g * jax.nn.sigmoid(SWIGLU_ALPHA * g) * (u + 1.0)
        y = _dot(hmid.astype(MXU_DTYPE), wd_ref[...]) + bd_ref[...]
        y_hi = y.astype(MXU_DTYPE)
        y_lo = (y - y_hi.astype(f32)).astype(MXU_DTYPE)
        scatter = jnp.where(rcol - lo == lan, 1.0, 0.0).astype(MXU_DTYPE)
        back = _dot(jnp.concatenate([scatter, scatter], axis=1), jnp.concatenate([y_hi, y_lo], axis=0))
        o_ref[...] += gate * back
        return carry

    lax.fori_loop(0, (cnt + MOE_CHUNK - 1) // MOE_CHUNK, chunk, 0)


def _moe(x1b, cw, w_gate, b_gate, w_up, b_up, w_down, b_down, *, tm):
    n, d = x1b.shape
    ne = cw.shape[1]
    dff = w_gate.shape[2]
    row = lambda i, e: (i, 0)
    wmap = lambda i, e: (e, 0, 0)
    return pl.pallas_call(
        _moe_kernel,
        grid=(n // tm, ne),
        in_specs=[pl.BlockSpec((tm, d), row), pl.BlockSpec((tm, ne), row),
                  pl.BlockSpec((None, d, dff), wmap), pl.BlockSpec((None, 1, dff), wmap),
                  pl.BlockSpec((None, d, dff), wmap), pl.BlockSpec((None, 1, dff), wmap),
                  pl.BlockSpec((None, dff, d), wmap), pl.BlockSpec((None, 1, d), wmap)],
        out_specs=pl.BlockSpec((tm, d), row),
        out_shape=jax.ShapeDtypeStruct((n, d), f32),
        scratch_shapes=[pltpu.VMEM((tm, ne), f32), pltpu.VMEM((ne, tm), f32)],
        compiler_params=_cparams(("parallel", "arbitrary")),
        name="moe_experts",
    )(x1b, cw, w_gate.astype(MXU_DTYPE), b_gate.reshape(ne, 1, dff), w_up.astype(MXU_DTYPE),
      b_up.reshape(ne, 1, dff), w_down.astype(MXU_DTYPE), b_down.reshape(ne, 1, d))


def _final_kernel(x1_ref, ffn_ref, p_ref, g2_ref, b2_ref, wpg_ref, wpp_ref, y_ref, *, dn_alpha):
    x2 = _layer_norm(dn_alpha * x1_ref[...] + ffn_ref[...], g2_ref[...], b2_ref[...])
    gate = jax.nn.sigmoid(_dot(x2.astype(MXU_DTYPE), wpg_ref[...]))
    emb = _dot(p_ref[...].astype(MXU_DTYPE), wpp_ref[...])
    y_ref[...] = x2 + gate * emb


def _final(x1, ffn, p2d, ln_g, ln_b, w_ple_gate, w_ple_proj, *, tm, dn_alpha):
    n, d = x1.shape
    dp = p2d.shape[1]
    row = lambda i: (i, 0)
    const = lambda i: (0, 0)
    kern = functools.partial(_final_kernel, dn_alpha=dn_alpha)
    return pl.pallas_call(
        kern,
        grid=(n // tm,),
        in_specs=[pl.BlockSpec((tm, d), row), pl.BlockSpec((tm, d), row), pl.BlockSpec((tm, dp), row),
                  pl.BlockSpec((1, d), const), pl.BlockSpec((1, d), const),
                  pl.BlockSpec((d, d), const), pl.BlockSpec((dp, d), const)],
        out_specs=pl.BlockSpec((tm, d), row),
        out_shape=jax.ShapeDtypeStruct((n, d), f32),
        compiler_params=_cparams(("parallel",)),
        name="ln2_ple",
    )(x1, ffn, p2d, ln_g.reshape(1, d), ln_b.reshape(1, d), w_ple_gate.astype(MXU_DTYPE),
      w_ple_proj.astype(MXU_DTYPE))


def _pick_tile(n, pref):
    t = min(pref, n)
    while n % t:
        t //= 2
    return t


def _layer(x, p, pos0, past_k, past_v, past_ki, conv_state, h0, prm, depth):
    (w_in, w_conv, b_conv, w_a, b_a, w_x, b_x, lam, w_out, ln1_g, ln1_b, w_router, b_router,
     w_gate, b_gate, w_up, b_up, w_down, b_down, ln2_g, ln2_b, w_ple_gate, w_ple_proj) = prm
    bsz, t, d = x.shape
    n = bsz * t
    past = past_k.shape[1]
    dn_alpha = (2 * depth) ** 0.25

    tm = _pick_tile(n, 512)
    pos = pos0 + jnp.arange(t)
    tabs_a = _rope_tables(pos, ROPE_DIM, HEAD_DIM)
    tabs_i = _rope_tables(pos, IDX_ROPE_DIM, IDX_DIM)
    if t % tm:
        rep = tm // t
        tabs_a = tuple(jnp.tile(a, (rep, 1)) for a in tabs_a)
        tabs_i = tuple(jnp.tile(a, (rep, 1)) for a in tabs_i)
    (q, k, kb, v, vb, qi, ki, kib, wi, xr, gr, ga, gb) = _in_projection(
        x.reshape(n, d), _pack_w_in(w_in), tabs_a, tabs_i, tm)

    s_true = past + t
    qb = min(Q_BLOCK, t)
    kblk = 1024 if s_true % 1024 == 0 else 3 * LANES
    s_pad = -(-s_true // kblk) * kblk

    def keys_by_position(past_arr, new_arr):
        new_arr = new_arr.reshape(bsz, t, -1)
        parts = [new_arr]
        if past:
            parts.insert(0, past_arr.reshape(bsz, past, new_arr.shape[2]).astype(MXU_DTYPE))
        if s_pad > s_true:
            parts.append(jnp.zeros((bsz, s_pad - s_true, new_arr.shape[2]), MXU_DTYPE))
        return jnp.concatenate(parts, axis=1)

    attn = _dsa_attention(q.reshape(bsz, t, -1), qi.reshape(bsz, t, -1), wi.reshape(bsz, t, -1),
                          keys_by_position(past_ki, kib), keys_by_position(past_k, kb),
                          keys_by_position(past_v, vb), qb=qb, kb=kblk, s_true=s_true, pos0=pos0)

    tt = _pick_tile(t, 512)
    xr3 = xr.reshape(bsz, t, -1)
    rnn, h_last = _rglru(xr3, gr.reshape(bsz, t, -1), conv_state, h0, w_conv, b_conv, w_a, b_a, w_x, b_x,
                         lam, tt=tt, reset_first=(pos0 == 0))
    new_conv = jnp.concatenate([conv_state, xr3], axis=1)[:, -(CONV_W - 1):]

    x1, x1b, cw = _merge_ln_router(attn.reshape(n, d), rnn.reshape(n, d), ga, gb, x.reshape(n, d), w_out,
                                   ln1_g, ln1_b, w_router, b_router, tm=tm, dn_alpha=dn_alpha)
    ffn = _moe(x1b, cw, w_gate, b_gate, w_up, b_up, w_down, b_down, tm=_pick_tile(n, 1024))
    y = _final(x1, ffn, p.reshape(n, -1), ln2_g, ln2_b, w_ple_gate, w_ple_proj, tm=tm, dn_alpha=dn_alpha)

    return (y.reshape(bsz, t, d), k.reshape(bsz, t, N_KV_HEADS, HEAD_DIM), v.reshape(bsz, t, N_KV_HEADS, HEAD_DIM),
            ki.reshape(bsz, t, IDX_DIM), new_conv, h_last.reshape(bsz, -1))


def kernel(x_prompt, x_sample, cache_k, cache_v, cache_kidx, state_conv, state_h, p_prompt, p_sample,
           w_in, w_conv, b_conv, w_a, b_a, w_x, b_x, lru_lambda, w_out, ln1_g, ln1_b, w_router, b_router,
           w_gate, b_gate, w_up, b_up, w_down, b_down, ln2_g, ln2_b, w_ple_gate, w_ple_proj):
    depth = w_in.shape[0]
    bp = x_prompt.shape[0]
    past_len = cache_k.shape[2]
    dt = x_prompt.dtype
    empty_kv = jnp.zeros((bp, 0, N_KV_HEADS, HEAD_DIM), dt)
    empty_ki = jnp.zeros((bp, 0, IDX_DIM), dt)
    zero_conv = jnp.zeros((bp, CONV_W - 1, x_prompt.shape[2]), dt)
    zero_h = jnp.zeros((bp, x_prompt.shape[2]), dt)

    yp, ys = x_prompt, x_sample
    outs_p, outs_s = [], []
    for l in range(depth):
        prm = (w_in[l], w_conv[l], b_conv[l], w_a[l], b_a[l], w_x[l], b_x[l], lru_lambda[l], w_out[l],
               ln1_g[l], ln1_b[l], w_router[l], b_router[l], w_gate[l], b_gate[l], w_up[l], b_up[l],
               w_down[l], b_down[l], ln2_g[l], ln2_b[l], w_ple_gate[l], w_ple_proj[l])
        yp, *rest_p = _layer(yp, p_prompt[l], 0, empty_kv, empty_kv, empty_ki, zero_conv, zero_h, prm, depth)
        ys, *rest_s = _layer(ys, p_sample[l], past_len, cache_k[l], cache_v[l], cache_kidx[l],
                             state_conv[l], state_h[l], prm, depth)
        outs_p.append(rest_p)
        outs_s.append(rest_s)

    stack = lambda outs, j: jnp.stack([o[j] for o in outs])
    return (yp, ys) + tuple(stack(outs_p, j) for j in range(5)) + tuple(stack(outs_s, j) for j in range(5))
```

```python
import functools

import jax
import jax.numpy as jnp
import numpy as np
from jax import lax
from jax.experimental import pallas as pl
from jax.experimental.pallas import tpu as pltpu

f32 = jnp.float32
i32 = jnp.int32
MXU_DTYPE = jnp.bfloat16

CHUNK = 64
HEAD_DIM = 128
N_KV_HEADS = 2
GROUP = 4
ROPE_DIM = 32
ROPE_THETA = 500000.0
N_IDX_HEADS = 8
IDX_DIM = 64
IDX_ROPE_DIM = 16
IDX_SCALE = (IDX_DIM * N_IDX_HEADS) ** -0.5
TOPK_KEYS = 256
Q_BLOCK = 128
N_RNN_BLOCKS = 16
RNN_BLOCK = 64
RNN_GROUP = 256
CONV_W = 4
LRU_C = 8.0
N_EXPERTS = 32
TOP_K = 4
SWIGLU_LIMIT = 7.0
SWIGLU_ALPHA = 1.702
LN_EPS = 1e-5

LANES = 128
SUBLANES = 8
VMEM_LIMIT = 56 * 1024 * 1024
INT_MIN = -2 ** 31
NEG_INF = float("-inf")
LOG2_E = 1.4426950408889634


def _cparams(sem):
    return pltpu.CompilerParams(dimension_semantics=sem, vmem_limit_bytes=VMEM_LIMIT)


def _resident(shape, index_map):
    return pl.BlockSpec(shape, index_map, pipeline_mode=pl.Buffered(1))


def _dot(a, b):
    return jnp.dot(a, b, preferred_element_type=f32)


_C_Q, _C_K, _C_V, _C_QI, _C_KW, _C_XR, _C_GR, _C_GA, _C_GB, _C_END = (
    0, 1024, 1280, 1536, 2048, 2176, 3200, 4224, 5248, 6272)


def _rope_tile(h, c, s_lo, s_hi, half):
    return h * c + pltpu.roll(h, half, 1) * s_hi + pltpu.roll(h, LANES - half, 1) * s_lo


def _inproj_kernel(x_ref, w_ref, ca_ref, sla_ref, sha_ref, ci_ref, sli_ref, shi_ref,
                   q_ref, k_ref, kb_ref, v_ref, vb_ref, qi_ref, ki_ref, kib_ref, wi_ref,
                   xr_ref, gr_ref, ga_ref, gb_ref):
    xb = x_ref[...].astype(MXU_DTYPE)

    def proj(a, b):
        return _dot(xb, w_ref[:, a:b])

    ca, sla, sha = ca_ref[...], sla_ref[...], sha_ref[...]
    ci, sli, shi = ci_ref[...], sli_ref[...], shi_ref[...]

    hq = proj(_C_Q, _C_K)
    for j in range(hq.shape[1] // LANES):
        t = _rope_tile(hq[:, j * LANES:(j + 1) * LANES], ca, sla, sha, ROPE_DIM // 2)
        q_ref[:, j * LANES:(j + 1) * LANES] = t.astype(q_ref.dtype)
    hk = proj(_C_K, _C_V)
    for j in range(hk.shape[1] // LANES):
        t = _rope_tile(hk[:, j * LANES:(j + 1) * LANES], ca, sla, sha, ROPE_DIM // 2)
        k_ref[:, j * LANES:(j + 1) * LANES] = t
        kb_ref[:, j * LANES:(j + 1) * LANES] = t.astype(kb_ref.dtype)
    hv = proj(_C_V, _C_QI)
    v_ref[...] = hv
    vb_ref[...] = hv.astype(vb_ref.dtype)
    hqi = proj(_C_QI, _C_KW)
    for j in range(hqi.shape[1] // LANES):
        t = _rope_tile(hqi[:, j * LANES:(j + 1) * LANES], ci, sli, shi, IDX_ROPE_DIM // 2)
        qi_ref[:, j * LANES:(j + 1) * LANES] = t.astype(qi_ref.dtype)
    hkw = proj(_C_KW, _C_XR)
    t = _rope_tile(hkw, ci, sli, shi, IDX_ROPE_DIM // 2)
    ki_ref[...] = t[:, :IDX_DIM]
    kib_ref[...] = t[:, :IDX_DIM].astype(kib_ref.dtype)
    wi_ref[...] = hkw[:, IDX_DIM:IDX_DIM + N_IDX_HEADS]
    xr_ref[...] = proj(_C_XR, _C_GR)
    gr_ref[...] = proj(_C_GR, _C_GA)
    ga_ref[...] = proj(_C_GA, _C_GB)
    gb_ref[...] = proj(_C_GB, _C_END)


def _rope_tables(pos, rot_dim, head_dim):
    half = rot_dim // 2
    inv = ROPE_THETA ** (-jnp.arange(half, dtype=f32) / half)
    ang = pos.astype(f32)[:, None] * inv[None, :]
    cos, sin = jnp.cos(ang), jnp.sin(ang)
    n = pos.shape[0]
    one = jnp.ones((n, head_dim - rot_dim), f32)
    zero_h = jnp.zeros((n, half), f32)
    zero_r = jnp.zeros((n, head_dim - rot_dim), f32)
    c = jnp.concatenate([cos, cos, one], axis=1)
    s_lo = jnp.concatenate([-sin, zero_h, zero_r], axis=1)
    s_hi = jnp.concatenate([zero_h, sin, zero_r], axis=1)
    rep = LANES // head_dim
    return tuple(jnp.tile(t, (1, rep)) for t in (c, s_lo, s_hi))


def _pack_w_in(w_in):
    d = w_in.shape[0]
    pad = jnp.zeros((d, LANES - IDX_DIM - N_IDX_HEADS), w_in.dtype)
    split = 1024 + 256 + 256 + 512 + IDX_DIM + N_IDX_HEADS
    return jnp.concatenate([w_in[:, :split], pad, w_in[:, split:]], axis=1).astype(MXU_DTYPE)


def _in_projection(x2d, w_packed, tabs_a, tabs_i, tm):
    n, d = x2d.shape
    nt = tabs_a[0].shape[0] // tm
    row = lambda i: (i, 0)
    tab = lambda i: (i % nt, 0)
    out_cols = [(1024, MXU_DTYPE), (256, f32), (256, MXU_DTYPE), (256, f32), (256, MXU_DTYPE),
                (512, MXU_DTYPE), (IDX_DIM, f32), (IDX_DIM, MXU_DTYPE), (N_IDX_HEADS, f32),
                (1024, f32), (1024, f32), (1024, f32), (1024, f32)]
    return pl.pallas_call(
        _inproj_kernel,
        grid=(n // tm,),
        in_specs=[pl.BlockSpec((tm, d), row), _resident(w_packed.shape, lambda i: (0, 0))]
                 + [pl.BlockSpec((tm, LANES), tab)] * 6,
        out_specs=[pl.BlockSpec((tm, c), row) for c, _ in out_cols],
        out_shape=[jax.ShapeDtypeStruct((n, c), dt) for c, dt in out_cols],
        compiler_params=_cparams(("parallel",)),
        name="in_projection",
    )(x2d, w_packed, *tabs_a, *tabs_i)


MASKED_LOGIT = -2.0 ** 126
MAX_FLOOR = -2.0 ** 120
UNKNOWN_COUNT = 1e9
MIN_NORMAL = 2.0 ** -126


def _dsa_kernel(q_ref, qi_ref, wi_ref, kit_ref, kt_ref, v_ref, o_ref,
                keys_scr, coarse_scr, kaug_scr, sa_scr, sb_scr, m_scr, l_scr, acc_scr, j_scr,
                *, qb, kb, s_true, s_pad, pos0, n_sel):
    i = pl.program_id(1)
    row = lax.broadcasted_iota(i32, (qb, 1), 0)
    pos = pos0 + i * qb + row
    vis_end = jnp.minimum((pos // CHUNK + 1) * CHUNK, s_true)
    pos_last = pos0 + i * qb + (qb - 1)
    kend = jnp.minimum((pos_last // CHUNK + 1) * CHUNK, s_true)
    nkb = (kend + kb - 1) // kb
    lane = lax.broadcasted_iota(i32, (qb, kb), 1)

    qi = qi_ref[...]
    wi = wi_ref[...]

    def score_block(b, carry):
        kit = kit_ref[b]
        sc = jnp.zeros((qb, kb), f32)
        for h in range(N_IDX_HEADS):
            d = _dot(qi[:, h * IDX_DIM:(h + 1) * IDX_DIM], kit)
            sc = sc + wi[:, h:h + 1] * jnp.maximum(d, 0.0)
        sc = sc * IDX_SCALE
        sc = jnp.where(jnp.abs(sc) < MIN_NORMAL, 0.0, sc)
        bits = pltpu.bitcast(sc, i32)
        key = bits ^ ((bits >> 31) & 0x7FFFFFFF)
        adm = b * kb + lane < vis_end
        keys_scr[b] = jnp.where(adm, key, INT_MIN)
        coarse = pltpu.bitcast(bits & -65536, f32)
        coarse_scr[b] = jnp.where(adm, coarse, NEG_INF).astype(jnp.bfloat16)
        return carry

    lax.fori_loop(0, nkb, score_block, 0)

    def count(pred):
        def body(b, acc):
            ind = jnp.where(pred(keys_scr[b], b), 1.0, 0.0)
            part = ind[:, 0:LANES]
            for t in range(1, kb // LANES):
                part = part + ind[:, t * LANES:(t + 1) * LANES]
            return acc + part
        acc = lax.fori_loop(0, nkb, body, jnp.zeros((qb, LANES), f32))
        return jnp.sum(acc, axis=1, keepdims=True)

    def count_coarse(cand16):
        pat = cand16 ^ ((cand16 >> 15) & 0x7FFF)
        val = pltpu.bitcast(jnp.left_shift(pat, 16), f32)
        val = jnp.where((cand16 > 0) & (cand16 < 128), MIN_NORMAL, val)
        cand =jnp.broadcast_to(val, (qb, LANES)).astype(jnp.bfloat16)
        one = jnp.ones((qb, LANES), jnp.bfloat16)
        zero = jnp.zeros((qb, LANES), jnp.bfloat16)

        def body(b, acc):
            x = coarse_scr[b]
            part = jnp.where(x[:, 0:LANES] >= cand, one, zero)
            for t in range(1, kb // LANES):
                part = part + jnp.where(x[:, t * LANES:(t + 1) * LANES] >= cand, one, zero)
            return acc + part.astype(f32)
        acc = lax.fori_loop(0, nkb, body, jnp.zeros((qb, LANES), f32))
        return jnp.sum(acc, axis=1, keepdims=True)

    nsel = jnp.float32(n_sel)

    def coarse_step(t, carry):
        tau16, ct = carry
        cand = tau16 + jnp.left_shift(jnp.int32(1), 15 - t)
        c = count_coarse(cand)
        ok = c >= nsel
        return jnp.where(ok, cand, tau16), jnp.where(ok, c, ct)

    tau16, ctau = lax.fori_loop(0, 16, coarse_step,
                                (jnp.full((qb, 1), -2 ** 15, i32), jnp.full((qb, 1), UNKNOWN_COUNT, f32)))

    def unsettled(ct):
        return jnp.max(jnp.abs(ct - nsel)) > 0.0

    def fine_cond(carry):
        t, _, ct = carry
        return (t < 16) & unsettled(ct)

    def fine_step(carry):
        t, tau, ct = carry
        cand = tau + jnp.left_shift(jnp.int32(1), 15 - t)
        c = count(lambda k, b: k >= cand)
        ok = c >= nsel
        return t + 1, jnp.where(ok, cand, tau), jnp.where(ok, c, ct)

    _, tau, ctau = lax.while_loop(fine_cond, fine_step, (jnp.int32(0), jnp.left_shift(tau16, 16), ctau))

    j_scr[...] = jnp.full((qb, 1), s_pad, i32)

    @pl.when(unsettled(ctau))
    def _():
        need = nsel - count(lambda k, b: k > tau)
        c_eq = count(lambda k, b: k == tau)

        @pl.when(jnp.max(jnp.where(tau == INT_MIN, 0.0, c_eq - need)) > 0.0)
        def _():
            def jbit(t, jlo):
                cand = jlo + jnp.left_shift(jnp.int32(1), (s_pad - 1).bit_length() - 1 - t)
                c = count(lambda k, b: (k == tau) & (b * kb + lane < cand))
                return jnp.where(c < need, cand, jlo)
            jlo = lax.fori_loop(0, (s_pad - 1).bit_length(), jbit, jnp.zeros((qb, 1), i32))
            j_scr[...] = jlo + 1

    jcut = jnp.where(tau == INT_MIN, 0, j_scr[...])

    q = q_ref[...]
    rq = lax.broadcasted_iota(i32, (GROUP * qb, qb), 0) % qb
    onehot = jnp.where(rq == lax.broadcasted_iota(i32, (GROUP * qb, qb), 1), 1.0, 0.0).astype(MXU_DTYPE)
    qaug = [jnp.concatenate([jnp.concatenate([q[:, (g * GROUP + j) * HEAD_DIM:(g * GROUP + j + 1) * HEAD_DIM]
                                              for j in range(GROUP)], axis=0), onehot], axis=1)
            for g in range(N_KV_HEADS)]
    m_scr[...] = jnp.full(m_scr.shape, MAX_FLOOR, f32)
    l_scr[...] = jnp.zeros(l_scr.shape, f32)
    acc_scr[...] = jnp.zeros(acc_scr.shape, f32)
    c2 = HEAD_DIM ** -0.5 * LOG2_E

    def logits(b, s_ref):
        k = keys_scr[b]
        selected = (k > tau) | ((k == tau) & (b * kb + lane < jcut))
        mask_rows = jnp.where(selected, 0.0, MASKED_LOGIT).astype(MXU_DTYPE)
        for g in range(N_KV_HEADS):
            kaug_scr[g, :HEAD_DIM, :] = kt_ref[b, g * HEAD_DIM:(g + 1) * HEAD_DIM, :]
            kaug_scr[g, HEAD_DIM:, :] = mask_rows
        for g in range(N_KV_HEADS):
            s_ref[g] = _dot(qaug[g], kaug_scr[g]) * c2

    def update(b, s_ref):
        for g in range(N_KV_HEADS):
            s = s_ref[g]
            m_old = m_scr[g]
            m_new = jnp.maximum(m_old, jnp.max(s, axis=1, keepdims=True))
            p = jnp.exp2(s - m_new)
            alpha = jnp.exp2(m_old - m_new)
            l_scr[g] = alpha * l_scr[g] + jnp.sum(p, axis=1, keepdims=True)
            pv = _dot(p.astype(MXU_DTYPE), v_ref[b, :, g * HEAD_DIM:(g + 1) * HEAD_DIM])
            acc_scr[g] = alpha * acc_scr[g] + pv
            m_scr[g] = m_new

    logits(0, sa_scr)

    def block_pair(ip, carry):
        b0 = 2 * ip
        logits(jnp.minimum(b0 + 1, nkb - 1), sb_scr)
        update(b0, sa_scr)

        @pl.when(b0 + 1 < nkb)
        def _():
            logits(jnp.minimum(b0 + 2, nkb - 1), sa_scr)
            update(b0 + 1, sb_scr)
        return carry

    lax.fori_loop(0, (nkb + 1) // 2, block_pair, 0)

    for g in range(N_KV_HEADS):
        o = acc_scr[g] / l_scr[g]
        for j in range(GROUP):
            h = g * GROUP + j
            o_ref[:, h * HEAD_DIM:(h + 1) * HEAD_DIM] = o[j * qb:(j + 1) * qb, :]


def _dsa_attention(q, qi, wi, ki_all, k_all, v_all, *, qb, kb, s_true, pos0):
    bsz, t, _ = q.shape
    s_pad = k_all.shape[1]
    nk = s_pad // kb
    n_sel = min(TOPK_KEYS, s_true // 4)
    kit = ki_all.reshape(bsz, nk, kb, IDX_DIM).transpose(0, 1, 3, 2)
    kt = k_all.reshape(bsz, nk, kb, N_KV_HEADS * HEAD_DIM).transpose(0, 1, 3, 2)
    vv = v_all.reshape(bsz, nk, kb, N_KV_HEADS * HEAD_DIM)
    qrow = lambda b, i: (b, i, 0)
    kmap = lambda b, i: (b, 0, 0, 0)
    kern = functools.partial(_dsa_kernel, qb=qb, kb=kb, s_true=s_true, s_pad=s_pad, pos0=pos0, n_sel=n_sel)
    return pl.pallas_call(
        kern,
        grid=(bsz, t // qb),
        in_specs=[pl.BlockSpec((None, qb, q.shape[2]), qrow),
                  pl.BlockSpec((None, qb, qi.shape[2]), qrow),
                  pl.BlockSpec((None, qb, wi.shape[2]), qrow),
                  _resident((None, nk, IDX_DIM, kb), kmap),
                  _resident((None, nk, N_KV_HEADS * HEAD_DIM, kb), kmap),
                  _resident((None, nk, kb, N_KV_HEADS * HEAD_DIM), kmap)],
        out_specs=pl.BlockSpec((None, qb, q.shape[2]), qrow),
        out_shape=jax.ShapeDtypeStruct(q.shape, f32),
        scratch_shapes=[pltpu.VMEM((nk, qb, kb), i32),
                        pltpu.VMEM((nk, qb, kb), jnp.bfloat16),
                        pltpu.VMEM((N_KV_HEADS, HEAD_DIM + qb, kb), MXU_DTYPE),
                        pltpu.VMEM((N_KV_HEADS, GROUP * qb, kb), f32),
                        pltpu.VMEM((N_KV_HEADS, GROUP * qb, kb), f32),
                        pltpu.VMEM((N_KV_HEADS, GROUP * qb, 1), f32),
                        pltpu.VMEM((N_KV_HEADS, GROUP * qb, 1), f32),
                        pltpu.VMEM((N_KV_HEADS, GROUP * qb, HEAD_DIM), f32),
                        pltpu.VMEM((qb, 1), i32)],
        compiler_params=_cparams(("parallel", "arbitrary")),
        name="dsa_attention",
    )(q, qi, wi, kit, kt, vv)


def _expm1(y):
    u = jnp.exp(y)
    um1 = u - 1.0
    return jnp.where(um1 == 0.0, y, jnp.where(um1 == -1.0, -1.0, um1 * y / jnp.log(u)))


def _rglru_kernel(xr_ref, gr_ref, cs_ref, h0_ref, wc_ref, bc_ref, wax_ref, ba_ref, bx_ref, lam_ref,
                  rnn_ref, hlast_ref, xp_scr, a_scr, b_scr, h_scr, *, reset_first):
    t = pl.program_id(1)
    tt, c = xr_ref.shape
    lead = SUBLANES

    @pl.when(t == 0)
    def _():
        xp_scr[lead - (CONV_W - 1):lead, :] = cs_ref[...]
        h_scr[...] = h0_ref[...]

    xp_scr[lead:lead + tt, :] = xr_ref[...]
    conv = bc_ref[...]
    for j in range(CONV_W):
        off = lead - (CONV_W - 1) + j
        conv = conv + xp_scr[off:off + tt, :] * wc_ref[j:j + 1, :]
    xp_scr[lead - (CONV_W - 1):lead, :] = xp_scr[lead + tt - (CONV_W - 1):lead + tt, :]

    cb = conv.astype(MXU_DTYPE)
    sp = lam_ref[...]
    grow = t * tt + lax.broadcasted_iota(i32, (tt, 1), 0)
    for g in range(c // RNN_GROUP):
        sl = slice(g * RNN_GROUP, (g + 1) * RNN_GROUP)
        z = _dot(cb[:, sl], wax_ref[g])
        r = jax.nn.sigmoid(z[:, :RNN_GROUP] + ba_ref[:, sl])
        ig = jax.nn.sigmoid(z[:, RNN_GROUP:] + bx_ref[:, sl])
        log_a = -LRU_C * r * sp[:, sl]
        mult = jnp.sqrt(-_expm1(2.0 * log_a))
        if reset_first:
            mult = jnp.where(grow == 0, 1.0, mult)
        a_scr[:, sl] = jnp.exp(log_a)
        b_scr[:, sl] = mult * (ig * conv[:, sl])

    def rows(jb, h):
        base = pl.multiple_of(jb * SUBLANES, SUBLANES)
        for u in range(SUBLANES):
            h = a_scr[pl.ds(base + u, 1), :] * h + b_scr[pl.ds(base + u, 1), :]
            a_scr[pl.ds(base + u, 1), :] = h
        return h

    h = lax.fori_loop(0, tt // SUBLANES, rows, h_scr[...])
    h_scr[...] = h
    rnn_ref[...] = a_scr[...] * jax.nn.gelu(gr_ref[...])

    @pl.when(t == pl.num_programs(1) - 1)
    def _():
        hlast_ref[...] = h


def _blockdiag_groups(w):
    per = RNN_GROUP // RNN_BLOCK
    g = w.reshape(N_RNN_BLOCKS // per, per, RNN_BLOCK, RNN_BLOCK)
    eye = jnp.eye(per, dtype=w.dtype)
    return jnp.einsum('gacd,ab->gacbd', g, eye).reshape(N_RNN_BLOCKS // per, RNN_GROUP, RNN_GROUP)


def _rglru(xr, gr, conv_state, h0, w_conv, b_conv, w_a, b_a, w_x, b_x, lam, *, tt, reset_first):
    bsz, t, c = xr.shape
    wax = jnp.concatenate([_blockdiag_groups(w_a), _blockdiag_groups(w_x)], axis=2).astype(MXU_DTYPE)
    sp = jax.nn.softplus(-lam.astype(f32)).reshape(1, c)
    seq = lambda b, i: (b, i, 0)
    per_b = lambda b, i: (b, 0, 0)
    const2 = lambda b, i: (0, 0)
    const3 = lambda b, i: (0, 0, 0)
    kern = functools.partial(_rglru_kernel, reset_first=reset_first)
    return pl.pallas_call(
        kern,
        grid=(bsz, t // tt),
        in_specs=[pl.BlockSpec((None, tt, c), seq), pl.BlockSpec((None, tt, c), seq),
                  pl.BlockSpec((None, CONV_W - 1, c), per_b), pl.BlockSpec((None, 1, c), per_b),
                  pl.BlockSpec((CONV_W, c), const2), pl.BlockSpec((1, c), const2),
                  pl.BlockSpec(wax.shape, const3), pl.BlockSpec((1, c), const2),
                  pl.BlockSpec((1, c), const2), pl.BlockSpec((1, c), const2)],
        out_specs=[pl.BlockSpec((None, tt, c), seq), pl.BlockSpec((None, 1, c), per_b)],
        out_shape=[jax.ShapeDtypeStruct((bsz, t, c), f32), jax.ShapeDtypeStruct((bsz, 1, c), f32)],
        scratch_shapes=[pltpu.VMEM((tt + SUBLANES, c), f32), pltpu.VMEM((tt, c), f32),
                        pltpu.VMEM((tt, c), f32), pltpu.VMEM((1, c), f32)],
        compiler_params=_cparams(("parallel", "arbitrary")),
        name="rglru",
    )(xr, gr, conv_state, h0.reshape(bsz, 1, c), w_conv, b_conv.reshape(1, c), wax,
      b_a.reshape(1, c), b_x.reshape(1, c), sp)


def _layer_norm(z, g, b):
    mu = jnp.mean(z, axis=-1, keepdims=True)
    var = jnp.mean(jnp.square(z - mu), axis=-1, keepdims=True)
    return (z - mu) * lax.rsqrt(var + LN_EPS) * g + b


def _merge_kernel(attn_ref, rnn_ref, ga_ref, gb_ref, x_ref, wo_ref, g1_ref, b1_ref, wr_ref, br_ref,
                  x1_ref, x1b_ref, cw_ref, *, dn_alpha):
    merged = jax.nn.sigmoid(ga_ref[...]) * attn_ref[...] + jax.nn.sigmoid(gb_ref[...]) * rnn_ref[...]
    m = _dot(merged.astype(MXU_DTYPE), wo_ref[...])
    x1 = _layer_norm(dn_alpha * x_ref[...] + m, g1_ref[...], b1_ref[...])
    x1_ref[...] = x1
    x1b = x1.astype(MXU_DTYPE)
    x1b_ref[...] = x1b
    logits = _dot(x1b, wr_ref[...]) + br_ref[...]
    ne = logits.shape[1]
    eidx = lax.broadcasted_iota(i32, logits.shape, 1).astype(f32)
    work = logits
    chosen = jnp.zeros(logits.shape, jnp.bool_)
    top = None
    for kk in range(TOP_K):
        mx = jnp.max(work, axis=1, keepdims=True)
        if kk == 0:
            top = mx
        first = jnp.min(jnp.where(work == mx, eidx, ne), axis=1, keepdims=True)
        pick = eidx == first
        chosen = chosen | pick
        work = jnp.where(pick, NEG_INF, work)
    e = jnp.where(chosen, jnp.exp(logits - top), 0.0)
    cw_ref[...] = e / jnp.sum(e, axis=1, keepdims=True)


def _merge_ln_router(attn, rnn, ga, gb, x2d, w_out, ln_g, ln_b, w_router, b_router, *, tm, dn_alpha):
    n, d = x2d.shape
    ne = w_router.shape[1]
    row = lambda i: (i, 0)
    const = lambda i: (0, 0)
    kern = functools.partial(_merge_kernel, dn_alpha=dn_alpha)
    return pl.pallas_call(
        kern,
        grid=(n // tm,),
        in_specs=[pl.BlockSpec((tm, d), row)] * 5
                 + [pl.BlockSpec((d, d), const), pl.BlockSpec((1, d), const), pl.BlockSpec((1, d), const),
                    pl.BlockSpec((d, ne), const), pl.BlockSpec((1, ne), const)],
        out_specs=[pl.BlockSpec((tm, d), row), pl.BlockSpec((tm, d), row), pl.BlockSpec((tm, ne), row)],
        out_shape=[jax.ShapeDtypeStruct((n, d), f32), jax.ShapeDtypeStruct((n, d), MXU_DTYPE),
                   jax.ShapeDtypeStruct((n, ne), f32)],
        compiler_params=_cparams(("parallel",)),
        name="merge_ln_router",
    )(attn, rnn, ga, gb, x2d, w_out.astype(MXU_DTYPE), ln_g.reshape(1, d), ln_b.reshape(1, d),
      w_router.astype(MXU_DTYPE), b_router.reshape(1, ne))


MOE_CHUNK = 128


def _moe_kernel(xb_ref, cw_ref, wg_ref, bg_ref, wu_ref, bu_ref, wd_ref, bd_ref, o_ref,
                rank_scr, rankt_scr):
    e = pl.program_id(1)
    tm, ne = cw_ref.shape

    @pl.when(e == 0)
    def _():
        sel = jnp.where(cw_ref[...] > 0.0, 1.0, 0.0).astype(MXU_DTYPE)
        r = lax.broadcasted_iota(i32, (tm, tm), 0)
        c = lax.broadcasted_iota(i32, (tm, tm), 1)
        lower = jnp.where(c < r, 1.0, 0.0).astype(MXU_DTYPE)
        upper = jnp.where(r < c, 1.0, 0.0).astype(MXU_DTYPE)
        rank = _dot(lower, sel)
        rank_scr[...] = jnp.where(sel > 0, rank, -1.0)
        eye = jnp.where(lax.broadcasted_iota(i32, (ne, ne), 0) == lax.broadcasted_iota(i32, (ne, ne), 1),
                        1.0, 0.0).astype(MXU_DTYPE)
        selt = lax.dot_general(eye, sel, (((1,), (1,)), ((), ())), preferred_element_type=f32)
        rankt = _dot(selt.astype(MXU_DTYPE), upper)
        rankt_scr[...] = jnp.where(selt > 0, rankt, -1.0)
        o_ref[...] = jnp.zeros(o_ref.shape, f32)

    rrow = rankt_scr[pl.ds(e, 1), :]
    emask = lax.broadcasted_iota(i32, (tm, ne), 1) == e
    rcol = jnp.sum(jnp.where(emask, rank_scr[...], 0.0), axis=1, keepdims=True)
    gate = jnp.sum(jnp.where(emask, cw_ref[...], 0.0), axis=1, keepdims=True)
    cnt = jnp.max(rrow).astype(i32) + 1
    sub = lax.broadcasted_iota(i32, (MOE_CHUNK, tm), 0).astype(f32)
    lan = lax.broadcasted_iota(i32, (tm, MOE_CHUNK), 1).astype(f32)

    def chunk(cidx, carry):
        lo = (cidx * MOE_CHUNK).astype(f32)
        gather = jnp.where(rrow - lo == sub, 1.0, 0.0).astype(MXU_DTYPE)
        xg = _dot(gather, xb_ref[...]).astype(MXU_DTYPE)
        g = _dot(xg, wg_ref[...]) + bg_ref[...]
        u = _dot(xg, wu_ref[...]) + bu_ref[...]
        g = jnp.minimum(g, SWIGLU_LIMIT)
        u = jnp.clip(u, -SWIGLU_LIMIT, SWIGLU_LIMIT)
        hmid = g * jax.nn.sigmoid(SWIGLU_ALPHA * g) * (u + 1.0)
        y = _dot(hmid.astype(MXU_DTYPE), wd_ref[...]) + bd_ref[...]
        y_hi = y.astype(MXU_DTYPE)
        y_lo = (y - y_hi.astype(f32)).astype(MXU_DTYPE)
        scatter = jnp.where(rcol - lo == lan, 1.0, 0.0).astype(MXU_DTYPE)
        back = _dot(jnp.concatenate([scatter, scatter], axis=1), jnp.concatenate([y_hi, y_lo], axis=0))
        o_ref[...] += gate * back
        return carry

    lax.fori_loop(0, (cnt + MOE_CHUNK - 1) // MOE_CHUNK, chunk, 0)


def _moe(x1b, cw, w_gate, b_gate, w_up, b_up, w_down, b_down, *, tm):
    n, d = x1b.shape
    ne = cw.shape[1]
    dff = w_gate.shape[2]
    row = lambda i, e: (i, 0)
    wmap = lambda i, e: (e, 0, 0)
    return pl.pallas_call(
        _moe_kernel,
        grid=(n // tm, ne),
        in_specs=[pl.BlockSpec((tm, d), row), pl.BlockSpec((tm, ne), row),
                  pl.BlockSpec((None, d, dff), wmap), pl.BlockSpec((None, 1, dff), wmap),
                  pl.BlockSpec((None, d, dff), wmap), pl.BlockSpec((None, 1, dff), wmap),
                  pl.BlockSpec((None, dff, d), wmap), pl.BlockSpec((None, 1, d), wmap)],
        out_specs=pl.BlockSpec((tm, d), row),
        out_shape=jax.ShapeDtypeStruct((n, d), f32),
        scratch_shapes=[pltpu.VMEM((tm, ne), f32), pltpu.VMEM((ne, tm), f32)],
        compiler_params=_cparams(("parallel", "arbitrary")),
        name="moe_experts",
    )(x1b, cw, w_gate.astype(MXU_DTYPE), b_gate.reshape(ne, 1, dff), w_up.astype(MXU_DTYPE),
      b_up.reshape(ne, 1, dff), w_down.astype(MXU_DTYPE), b_down.reshape(ne, 1, d))


def _final_kernel(x1_ref, ffn_ref, p_ref, g2_ref, b2_ref, wpg_ref, wpp_ref, y_ref, *, dn_alpha):
    x2 = _layer_norm(dn_alpha * x1_ref[...] + ffn_ref[...], g2_ref[...], b2_ref[...])
    gate = jax.nn.sigmoid(_dot(x2.astype(MXU_DTYPE), wpg_ref[...]))
    emb = _dot(p_ref[...].astype(MXU_DTYPE), wpp_ref[...])
    y_ref[...] = x2 + gate * emb


def _final(x1, ffn, p2d, ln_g, ln_b, w_ple_gate, w_ple_proj, *, tm, dn_alpha):
    n, d = x1.shape
    dp = p2d.shape[1]
    row = lambda i: (i, 0)
    const = lambda i: (0, 0)
    kern = functools.partial(_final_kernel, dn_alpha=dn_alpha)
    return pl.pallas_call(
        kern,
        grid=(n // tm,),
        in_specs=[pl.BlockSpec((tm, d), row), pl.BlockSpec((tm, d), row), pl.BlockSpec((tm, dp), row),
                  pl.BlockSpec((1, d), const), pl.BlockSpec((1, d), const),
                  pl.BlockSpec((d, d), const), pl.BlockSpec((dp, d), const)],
        out_specs=pl.BlockSpec((tm, d), row),
        out_shape=jax.ShapeDtypeStruct((n, d), f32),
        compiler_params=_cparams(("parallel",)),
        name="ln2_ple",
    )(x1, ffn, p2d, ln_g.reshape(1, d), ln_b.reshape(1, d), w_ple_gate.astype(MXU_DTYPE),
      w_ple_proj.astype(MXU_DTYPE))


def _pick_tile(n, pref):
    t = min(pref, n)
    while n % t:
        t //= 2
    return t


def _layer(x, p, pos0, past_k, past_v, past_ki, conv_state, h0, prm, depth):
    (w_in, w_conv, b_conv, w_a, b_a, w_x, b_x, lam, w_out, ln1_g, ln1_b, w_router, b_router,
     w_gate, b_gate, w_up, b_up, w_down, b_down, ln2_g, ln2_b, w_ple_gate, w_ple_proj) = prm
    bsz, t, d = x.shape
    n = bsz * t
    past = past_k.shape[1]
    dn_alpha = (2 * depth) ** 0.25

    tm = _pick_tile(n, 512)
    pos = pos0 + jnp.arange(t)
    tabs_a = _rope_tables(pos, ROPE_DIM, HEAD_DIM)
    tabs_i = _rope_tables(pos, IDX_ROPE_DIM, IDX_DIM)
    if t % tm:
        rep = tm // t
        tabs_a = tuple(jnp.tile(a, (rep, 1)) for a in tabs_a)
        tabs_i = tuple(jnp.tile(a, (rep, 1)) for a in tabs_i)
    (q, k, kb, v, vb, qi, ki, kib, wi, xr, gr, ga, gb) = _in_projection(
        x.reshape(n, d), _pack_w_in(w_in), tabs_a, tabs_i, tm)

    s_true = past + t
    qb = min(Q_BLOCK, t)
    kblk = 1024 if s_true % 1024 == 0 else 3 * LANES
    s_pad = -(-s_true // kblk) * kblk

    def keys_by_position(past_arr, new_arr):
        new_arr = new_arr.reshape(bsz, t, -1)
        parts = [new_arr]
        if past:
            parts.insert(0, past_arr.reshape(bsz, past, new_arr.shape[2]).astype(MXU_DTYPE))
        if s_pad > s_true:
            parts.append(jnp.zeros((bsz, s_pad - s_true, new_arr.shape[2]), MXU_DTYPE))
        return jnp.concatenate(parts, axis=1)

    attn = _dsa_attention(q.reshape(bsz, t, -1), qi.reshape(bsz, t, -1), wi.reshape(bsz, t, -1),
                          keys_by_position(past_ki, kib), keys_by_position(past_k, kb),
                          keys_by_position(past_v, vb), qb=qb, kb=kblk, s_true=s_true, pos0=pos0)

    tt = _pick_tile(t, 512)
    xr3 = xr.reshape(bsz, t, -1)
    rnn, h_last = _rglru(xr3, gr.reshape(bsz, t, -1), conv_state, h0, w_conv, b_conv, w_a, b_a, w_x, b_x,
                         lam, tt=tt, reset_first=(pos0 == 0))
    new_conv = jnp.concatenate([conv_state, xr3], axis=1)[:, -(CONV_W - 1):]

    x1, x1b, cw = _merge_ln_router(attn.reshape(n, d), rnn.reshape(n, d), ga, gb, x.reshape(n, d), w_out,
                                   ln1_g, ln1_b, w_router, b_router, tm=tm, dn_alpha=dn_alpha)
    ffn = _moe(x1b, cw, w_gate, b_gate, w_up, b_up, w_down, b_down, tm=_pick_tile(n, 1024))
    y = _final(x1, ffn, p.reshape(n, -1), ln2_g, ln2_b, w_ple_gate, w_ple_proj, tm=tm, dn_alpha=dn_alpha)

    return (y.reshape(bsz, t, d), k.reshape(bsz, t, N_KV_HEADS, HEAD_DIM), v.reshape(bsz, t, N_KV_HEADS, HEAD_DIM),
            ki.reshape(bsz, t, IDX_DIM), new_conv, h_last.reshape(bsz, -1))


def kernel(x_prompt, x_sample, cache_k, cache_v, cache_kidx, state_conv, state_h, p_prompt, p_sample,
           w_in, w_conv, b_conv, w_a, b_a, w_x, b_x, lru_lambda, w_out, ln1_g, ln1_b, w_router, b_router,
           w_gate, b_gate, w_up, b_up, w_down, b_down, ln2_g, ln2_b, w_ple_gate, w_ple_proj):
    depth = w_in.shape[0]
    bp = x_prompt.shape[0]
    past_len = cache_k.shape[2]
    dt = x_prompt.dtype
    empty_kv = jnp.zeros((bp, 0, N_KV_HEADS, HEAD_DIM), dt)
    empty_ki = jnp.zeros((bp, 0, IDX_DIM), dt)
    zero_conv = jnp.zeros((bp, CONV_W - 1, x_prompt.shape[2]), dt)
    zero_h = jnp.zeros((bp, x_prompt.shape[2]), dt)

    yp, ys = x_prompt, x_sample
    outs_p, outs_s = [], []
    for l in range(depth):
        prm = (w_in[l], w_conv[l], b_conv[l], w_a[l], b_a[l], w_x[l], b_x[l], lru_lambda[l], w_out[l],
               ln1_g[l], ln1_b[l], w_router[l], b_router[l], w_gate[l], b_gate[l], w_up[l], b_up[l],
               w_down[l], b_down[l], ln2_g[l], ln2_b[l], w_ple_gate[l], w_ple_proj[l])
        yp, *rest_p = _layer(yp, p_prompt[l], 0, empty_kv, empty_kv, empty_ki, zero_conv, zero_h, prm, depth)
        ys, *rest_s = _layer(ys, p_sample[l], past_len, cache_k[l], cache_v[l], cache_kidx[l],
                             state_conv[l], state_h[l], prm, depth)
        outs_p.append(rest_p)
        outs_s.append(rest_s)

    stack = lambda outs, j: jnp.stack([o[j] for o in outs])
    return (yp, ys) + tuple(stack(outs_p, j) for j in range(5)) + tuple(stack(outs_s, j) for j in range(5))
```

```python
import functools

import jax
import jax.numpy as jnp
import numpy as np
from jax import lax
from jax.experimental import pallas as pl
from jax.experimental.pallas import tpu as pltpu

f32 = jnp.float32
i32 = jnp.int32
MXU_DTYPE = jnp.bfloat16

CHUNK = 64
HEAD_DIM = 128
N_KV_HEADS = 2
GROUP = 4
ROPE_DIM = 32
ROPE_THETA = 500000.0
N_IDX_HEADS = 8
IDX_DIM = 64
IDX_ROPE_DIM = 16
IDX_SCALE = (IDX_DIM * N_IDX_HEADS) ** -0.5
TOPK_KEYS = 256
Q_BLOCK = 128
N_RNN_BLOCKS = 16
RNN_BLOCK = 64
RNN_GROUP = 256
CONV_W = 4
LRU_C = 8.0
N_EXPERTS = 32
TOP_K = 4
SWIGLU_LIMIT = 7.0
SWIGLU_ALPHA = 1.702
LN_EPS = 1e-5

LANES = 128
SUBLANES = 8
VMEM_LIMIT = 56 * 1024 * 1024
INT_MIN = -2 ** 31
NEG_INF = float("-inf")
LOG2_E = 1.4426950408889634


def _cparams(sem):
    return pltpu.CompilerParams(dimension_semantics=sem, vmem_limit_bytes=VMEM_LIMIT)


def _resident(shape, index_map):
    return pl.BlockSpec(shape, index_map, pipeline_mode=pl.Buffered(1))


def _dot(a, b):
    return jnp.dot(a, b, preferred_element_type=f32)


_C_Q, _C_K, _C_V, _C_QI, _C_KW, _C_XR, _C_GR, _C_GA, _C_GB, _C_END = (
    0, 1024, 1280, 1536, 2048, 2176, 3200, 4224, 5248, 6272)


def _rope_tile(h, c, s_lo, s_hi, half):
    return h * c + pltpu.roll(h, half, 1) * s_hi + pltpu.roll(h, LANES - half, 1) * s_lo


def _inproj_kernel(x_ref, w_ref, ca_ref, sla_ref, sha_ref, ci_ref, sli_ref, shi_ref,
                   q_ref, k_ref, kb_ref, v_ref, vb_ref, qi_ref, ki_ref, kib_ref, wi_ref,
                   xr_ref, gr_ref, ga_ref, gb_ref):
    xb = x_ref[...].astype(MXU_DTYPE)

    def proj(a, b):
        return _dot(xb, w_ref[:, a:b])

    ca, sla, sha = ca_ref[...], sla_ref[...], sha_ref[...]
    ci, sli, shi = ci_ref[...], sli_ref[...], shi_ref[...]

    hq = proj(_C_Q, _C_K)
    for j in range(hq.shape[1] // LANES):
        t = _rope_tile(hq[:, j * LANES:(j + 1) * LANES], ca, sla, sha, ROPE_DIM // 2)
        q_ref[:, j * LANES:(j + 1) * LANES] = t.astype(q_ref.dtype)
    hk = proj(_C_K, _C_V)
    for j in range(hk.shape[1] // LANES):
        t = _rope_tile(hk[:, j * LANES:(j + 1) * LANES], ca, sla, sha, ROPE_DIM // 2)
        k_ref[:, j * LANES:(j + 1) * LANES] = t
        kb_ref[:, j * LANES:(j + 1) * LANES] = t.astype(kb_ref.dtype)
    hv = proj(_C_V, _C_QI)
    v_ref[...] = hv
    vb_ref[...] = hv.astype(vb_ref.dtype)
    hqi = proj(_C_QI, _C_KW)
    for j in range(hqi.shape[1] // LANES):
        t = _rope_tile(hqi[:, j * LANES:(j + 1) * LANES], ci, sli, shi, IDX_ROPE_DIM // 2)
        qi_ref[:, j * LANES:(j + 1) * LANES] = t.astype(qi_ref.dtype)
    hkw = proj(_C_KW, _C_XR)
    t = _rope_tile(hkw, ci, sli, shi, IDX_ROPE_DIM // 2)
    ki_ref[...] = t[:, :IDX_DIM]
    kib_ref[...] = t[:, :IDX_DIM].astype(kib_ref.dtype)
    wi_ref[...] = hkw[:, IDX_DIM:IDX_DIM + N_IDX_HEADS]
    xr_ref[...] = proj(_C_XR, _C_GR)
    gr_ref[...] = proj(_C_GR, _C_GA)
    ga_ref[...] = proj(_C_GA, _C_GB)
    gb_ref[...] = proj(_C_GB, _C_END)


def _rope_tables(pos, rot_dim, head_dim):
    half = rot_dim // 2
    inv = ROPE_THETA ** (-jnp.arange(half, dtype=f32) / half)
    ang = pos.astype(f32)[:, None] * inv[None, :]
    cos, sin = jnp.cos(ang), jnp.sin(ang)
    n = pos.shape[0]
    one = jnp.ones((n, head_dim - rot_dim), f32)
    zero_h = jnp.zeros((n, half), f32)
    zero_r = jnp.zeros((n, head_dim - rot_dim), f32)
    c = jnp.concatenate([cos, cos, one], axis=1)
    s_lo = jnp.concatenate([-sin, zero_h, zero_r], axis=1)
    s_hi = jnp.concatenate([zero_h, sin, zero_r], axis=1)
    rep = LANES // head_dim
    return tuple(jnp.tile(t, (1, rep)) for t in (c, s_lo, s_hi))


def _pack_w_in(w_in):
    d = w_in.shape[0]
    pad = jnp.zeros((d, LANES - IDX_DIM - N_IDX_HEADS), w_in.dtype)
    split = 1024 + 256 + 256 + 512 + IDX_DIM + N_IDX_HEADS
    return jnp.concatenate([w_in[:, :split], pad, w_in[:, split:]], axis=1).astype(MXU_DTYPE)


def _in_projection(x2d, w_packed, tabs_a, tabs_i, tm):
    n, d = x2d.shape
    nt = tabs_a[0].shape[0] // tm
    row = lambda i: (i, 0)
    tab = lambda i: (i % nt, 0)
    out_cols = [(1024, MXU_DTYPE), (256, f32), (256, MXU_DTYPE), (256, f32), (256, MXU_DTYPE),
                (512, MXU_DTYPE), (IDX_DIM, f32), (IDX_DIM, MXU_DTYPE), (N_IDX_HEADS, f32),
                (1024, f32), (1024, f32), (1024, f32), (1024, f32)]
    return pl.pallas_call(
        _inproj_kernel,
        grid=(n // tm,),
        in_specs=[pl.BlockSpec((tm, d), row), _resident(w_packed.shape, lambda i: (0, 0))]
                 + [pl.BlockSpec((tm, LANES), tab)] * 6,
        out_specs=[pl.BlockSpec((tm, c), row) for c, _ in out_cols],
        out_shape=[jax.ShapeDtypeStruct((n, c), dt) for c, dt in out_cols],
        compiler_params=_cparams(("parallel",)),
        name="in_projection",
    )(x2d, w_packed, *tabs_a, *tabs_i)


MASKED_LOGIT = -2.0 ** 126
MAX_FLOOR = -2.0 ** 120
UNKNOWN_COUNT = 1e9
MIN_NORMAL = 2.0 ** -126


def _dsa_kernel(q_ref, qi_ref, wi_ref, kit_ref, kt_ref, v_ref, o_ref,
                keys_scr, kaug_scr, sa_scr, sb_scr, m_scr, l_scr, acc_scr, upper_scr,
                *, qb, kb, s_true, pos0, n_sel):
    i = pl.program_id(1)

    @pl.when(i == 0)
    def _():
        upper_scr[...] = jnp.where(lax.broadcasted_iota(i32, (kb, kb), 0) < lax.broadcasted_iota(i32, (kb, kb), 1),
                                   1.0, 0.0).astype(MXU_DTYPE)

    row = lax.broadcasted_iota(i32, (qb, 1), 0)
    pos = pos0 + i * qb + row
    vis_end = jnp.minimum((pos // CHUNK + 1) * CHUNK, s_true)
    pos_last = pos0 + i * qb + (qb - 1)
    kend = jnp.minimum((pos_last // CHUNK + 1) * CHUNK, s_true)
    nkb = (kend + kb - 1) // kb
    lane = lax.broadcasted_iota(i32, (qb, kb), 1)

    qi = qi_ref[...]
    wi = wi_ref[...]

    def score_block(b, carry):
        kit = kit_ref[b]
        sc = jnp.zeros((qb, kb), f32)
        for h in range(N_IDX_HEADS):
            d = _dot(qi[:, h * IDX_DIM:(h + 1) * IDX_DIM], kit)
            sc = sc + wi[:, h:h + 1] * jnp.maximum(d, 0.0)
        sc = sc * IDX_SCALE
        sc = jnp.where(jnp.abs(sc) < MIN_NORMAL, 0.0, sc)
        bits = pltpu.bitcast(sc, i32)
        key = bits ^ ((bits >> 31) & 0x7FFFFFFF)
        keys_scr[b] = jnp.where(b * kb + lane < vis_end, key, INT_MIN)
        return carry

    lax.fori_loop(0, nkb, score_block, 0)

    def count(pred):
        def body(b, acc):
            for t in range(kb // LANES):
                hit = pred(keys_scr[b, :, t * LANES:(t + 1) * LANES], b * kb + t * LANES)
                acc = acc + jnp.where(hit, 1.0, 0.0)
            return acc
        acc = lax.fori_loop(0, nkb, body, jnp.zeros((qb, LANES), f32))
        return jnp.sum(acc, axis=1, keepdims=True)

    def wide(x):
        return jnp.broadcast_to(x, (qb, LANES))

    nsel = jnp.float32(n_sel)

    def unsettled(ct):
        return jnp.max(jnp.abs(ct - nsel)) > 0.0

    def bisect_cond(carry):
        t, tau, ct, hi = carry
        open_rows = jnp.where((ct == nsel) | (hi == tau + 1), 0.0, 1.0)
        return (t < 33) & (jnp.max(open_rows) > 0.0)

    def bisect_one(t, tau, ct, hi):
        shift = jnp.clip(jnp.where(t == 0, 31, 32 - t), 0, 31)
        cand = tau + jnp.where((t == 1) | (t > 32), 1, jnp.left_shift(jnp.int32(1), shift))
        cw = wide(cand)
        c = count(lambda k, c0: k >= cw)
        ok = c >= nsel
        return jnp.where(ok, cand, tau), jnp.where(ok, c, ct), jnp.where(ok, hi, cand)

    def bisect_step(carry):
        t, tau, ct, hi = carry
        tau, ct, hi = bisect_one(t, tau, ct, hi)
        tau, ct, hi = bisect_one(t + 1, tau, ct, hi)
        return t + 2, tau, ct, hi

    _, tau, ctau, _ = lax.while_loop(bisect_cond, bisect_step,
                                     (jnp.int32(0), jnp.full((qb, 1), INT_MIN, i32),
                                      jnp.full((qb, 1), UNKNOWN_COUNT, f32), jnp.full((qb, 1), INT_MIN, i32)))

    @pl.when(unsettled(ctau))
    def _():
        tw = wide(tau)
        need = nsel - count(lambda k, c0: k > tw)
        need = jnp.where(tau == INT_MIN, UNKNOWN_COUNT, need)

        def demote(b, seen):
            k = keys_scr[b]
            eq = k == tau
            before = _dot(jnp.where(eq, 1.0, 0.0).astype(MXU_DTYPE), upper_scr[...]) + seen
            keys_scr[b] = jnp.where(eq & (before >= need), tau - 1, k)
            return seen + jnp.sum(jnp.where(eq, 1.0, 0.0), axis=1, keepdims=True)

        lax.fori_loop(0, nkb, demote, jnp.zeros((qb, 1), f32))

    tau_sel = jnp.maximum(tau, INT_MIN + 1)

    q = q_ref[...]
    rq = lax.broadcasted_iota(i32, (GROUP * qb, qb), 0) % qb
    onehot = jnp.where(rq == lax.broadcasted_iota(i32, (GROUP * qb, qb), 1), 1.0, 0.0).astype(MXU_DTYPE)
    qaug = [jnp.concatenate([jnp.concatenate([q[:, (g * GROUP + j) * HEAD_DIM:(g * GROUP + j + 1) * HEAD_DIM]
                                              for j in range(GROUP)], axis=0), onehot], axis=1)
            for g in range(N_KV_HEADS)]
    m_scr[...] = jnp.full(m_scr.shape, MAX_FLOOR, f32)
    l_scr[...] = jnp.zeros(l_scr.shape, f32)
    acc_scr[...] = jnp.zeros(acc_scr.shape, f32)
    c2 = HEAD_DIM ** -0.5 * LOG2_E

    def logits(b, s_ref):
        k = keys_scr[b]
        mask_rows = jnp.where(k >= tau_sel, 0.0, MASKED_LOGIT).astype(MXU_DTYPE)
        for g in range(N_KV_HEADS):
            kaug_scr[g, :HEAD_DIM, :] = kt_ref[b, g * HEAD_DIM:(g + 1) * HEAD_DIM, :]
            kaug_scr[g, HEAD_DIM:, :] = mask_rows
        for g in range(N_KV_HEADS):
            s_ref[g] = _dot(qaug[g], kaug_scr[g]) * c2

    def update(b, s_ref):
        for g in range(N_KV_HEADS):
            m_old = m_scr[g]
            m_new = jnp.maximum(m_old, jnp.max(s_ref[g], axis=1, keepdims=True))
            p = jnp.exp2(s_ref[g] - m_new)
            alpha = jnp.exp2(m_old - m_new)
            l_scr[g] = alpha * l_scr[g] + jnp.sum(p, axis=1, keepdims=True)
            pv = _dot(p.astype(MXU_DTYPE), v_ref[b, :, g * HEAD_DIM:(g + 1) * HEAD_DIM])
            acc_scr[g] = alpha * acc_scr[g] + pv
            m_scr[g] = m_new

    logits(0, sa_scr)

    def block_pair(ip, carry):
        b0 = 2 * ip
        logits(jnp.minimum(b0 + 1, nkb - 1), sb_scr)
        update(b0, sa_scr)

        @pl.when(b0 + 1 < nkb)
        def _():
            logits(jnp.minimum(b0 + 2, nkb - 1), sa_scr)
            update(b0 + 1, sb_scr)
        return carry

    lax.fori_loop(0, (nkb + 1) // 2, block_pair, 0)

    for g in range(N_KV_HEADS):
        o = acc_scr[g] / l_scr[g]
        for j in range(GROUP):
            h = g * GROUP + j
            o_ref[:, h * HEAD_DIM:(h + 1) * HEAD_DIM] = o[j * qb:(j + 1) * qb, :]


def _dsa_attention(q, qi, wi, ki_all, k_all, v_all, *, qb, kb, s_true, pos0):
    bsz, t, _ = q.shape
    s_pad = k_all.shape[1]
    nk = s_pad // kb
    n_sel = min(TOPK_KEYS, s_true // 4)
    kit = ki_all.reshape(bsz, nk, kb, IDX_DIM).transpose(0, 1, 3, 2)
    kt = k_all.reshape(bsz, nk, kb, N_KV_HEADS * HEAD_DIM).transpose(0, 1, 3, 2)
    vv = v_all.reshape(bsz, nk, kb, N_KV_HEADS * HEAD_DIM)
    qrow = lambda b, i: (b, i, 0)
    kmap = lambda b, i: (b, 0, 0, 0)
    kern = functools.partial(_dsa_kernel, qb=qb, kb=kb, s_true=s_true, pos0=pos0, n_sel=n_sel)
    return pl.pallas_call(
        kern,
        grid=(bsz, t // qb),
        in_specs=[pl.BlockSpec((None, qb, q.shape[2]), qrow),
                  pl.BlockSpec((None, qb, qi.shape[2]), qrow),
                  pl.BlockSpec((None, qb, wi.shape[2]), qrow),
                  _resident((None, nk, IDX_DIM, kb), kmap),
                  _resident((None, nk, N_KV_HEADS * HEAD_DIM, kb), kmap),
                  _resident((None, nk, kb, N_KV_HEADS * HEAD_DIM), kmap)],
        out_specs=pl.BlockSpec((None, qb, q.shape[2]), qrow),
        out_shape=jax.ShapeDtypeStruct(q.shape, f32),
        scratch_shapes=[pltpu.VMEM((nk, qb, kb), i32),
                        pltpu.VMEM((N_KV_HEADS, HEAD_DIM + qb, kb), MXU_DTYPE),
                        pltpu.VMEM((N_KV_HEADS, GROUP * qb, kb), f32),
                        pltpu.VMEM((N_KV_HEADS, GROUP * qb, kb), f32),
                        pltpu.VMEM((N_KV_HEADS, GROUP * qb, 1), f32),
                        pltpu.VMEM((N_KV_HEADS, GROUP * qb, 1), f32),
                        pltpu.VMEM((N_KV_HEADS, GROUP * qb, HEAD_DIM), f32),
                        pltpu.VMEM((kb, kb), MXU_DTYPE)],
        compiler_params=_cparams(("parallel", "arbitrary")),
        name="dsa_attention",
    )(q, qi, wi, kit, kt, vv)


def _expm1(y):
    u = jnp.exp(y)
    um1 = u - 1.0
    return jnp.where(um1 == 0.0, y, jnp.where(um1 == -1.0, -1.0, um1 * y / jnp.log(u)))


def _rglru_kernel(xr_ref, gr_ref, cs_ref, h0_ref, wc_ref, bc_ref, wax_ref, ba_ref, bx_ref, lam_ref,
                  rnn_ref, hlast_ref, xp_scr, a_scr, b_scr, h_scr, *, reset_first):
    t = pl.program_id(1)
    tt, c = xr_ref.shape
    lead = SUBLANES

    @pl.when(t == 0)
    def _():
        xp_scr[lead - (CONV_W - 1):lead, :] = cs_ref[...]
        h_scr[...] = h0_ref[...]

    xp_scr[lead:lead + tt, :] = xr_ref[...]
    conv = bc_ref[...]
    for j in range(CONV_W):
        off = lead - (CONV_W - 1) + j
        conv = conv + xp_scr[off:off + tt, :] * wc_ref[j:j + 1, :]
    xp_scr[lead - (CONV_W - 1):lead, :] = xp_scr[lead + tt - (CONV_W - 1):lead + tt, :]

    cb = conv.astype(MXU_DTYPE)
    sp = lam_ref[...]
    grow = t * tt + lax.broadcasted_iota(i32, (tt, 1), 0)
    for g in range(c // RNN_GROUP):
        sl = slice(g * RNN_GROUP, (g + 1) * RNN_GROUP)
        z = _dot(cb[:, sl], wax_ref[g])
        r = jax.nn.sigmoid(z[:, :RNN_GROUP] + ba_ref[:, sl])
        ig = jax.nn.sigmoid(z[:, RNN_GROUP:] + bx_ref[:, sl])
        log_a = -LRU_C * r * sp[:, sl]
        mult = jnp.sqrt(-_expm1(2.0 * log_a))
        if reset_first:
            mult = jnp.where(grow == 0, 1.0, mult)
        a_scr[:, sl] = jnp.exp(log_a)
        b_scr[:, sl] = mult * (ig * conv[:, sl])

    def rows(jb, h):
        base = pl.multiple_of(jb * SUBLANES, SUBLANES)
        for u in range(SUBLANES):
            h = a_scr[pl.ds(base + u, 1), :] * h + b_scr[pl.ds(base + u, 1), :]
            a_scr[pl.ds(base + u, 1), :] = h
        return h

    h = lax.fori_loop(0, tt // SUBLANES, rows, h_scr[...])
    h_scr[...] = h
    rnn_ref[...] = a_scr[...] * jax.nn.gelu(gr_ref[...])

    @pl.when(t == pl.num_programs(1) - 1)
    def _():
        hlast_ref[...] = h


def _blockdiag_groups(w):
    per = RNN_GROUP // RNN_BLOCK
    g = w.reshape(N_RNN_BLOCKS // per, per, RNN_BLOCK, RNN_BLOCK)
    eye = jnp.eye(per, dtype=w.dtype)
    return jnp.einsum('gacd,ab->gacbd', g, eye).reshape(N_RNN_BLOCKS // per, RNN_GROUP, RNN_GROUP)


def _rglru(xr, gr, conv_state, h0, w_conv, b_conv, w_a, b_a, w_x, b_x, lam, *, tt, reset_first):
    bsz, t, c = xr.shape
    wax = jnp.concatenate([_blockdiag_groups(w_a), _blockdiag_groups(w_x)], axis=2).astype(MXU_DTYPE)
    sp = jax.nn.softplus(-lam.astype(f32)).reshape(1, c)
    seq = lambda b, i: (b, i, 0)
    per_b = lambda b, i: (b, 0, 0)
    const2 = lambda b, i: (0, 0)
    const3 = lambda b, i: (0, 0, 0)
    kern = functools.partial(_rglru_kernel, reset_first=reset_first)
    return pl.pallas_call(
        kern,
        grid=(bsz, t // tt),
        in_specs=[pl.BlockSpec((None, tt, c), seq), pl.BlockSpec((None, tt, c), seq),
                  pl.BlockSpec((None, CONV_W - 1, c), per_b), pl.BlockSpec((None, 1, c), per_b),
                  pl.BlockSpec((CONV_W, c), const2), pl.BlockSpec((1, c), const2),
                  pl.BlockSpec(wax.shape, const3), pl.BlockSpec((1, c), const2),
                  pl.BlockSpec((1, c), const2), pl.BlockSpec((1, c), const2)],
        out_specs=[pl.BlockSpec((None, tt, c), seq), pl.BlockSpec((None, 1, c), per_b)],
        out_shape=[jax.ShapeDtypeStruct((bsz, t, c), f32), jax.ShapeDtypeStruct((bsz, 1, c), f32)],
        scratch_shapes=[pltpu.VMEM((tt + SUBLANES, c), f32), pltpu.VMEM((tt, c), f32),
                        pltpu.VMEM((tt, c), f32), pltpu.VMEM((1, c), f32)],
        compiler_params=_cparams(("parallel", "arbitrary")),
        name="rglru",
    )(xr, gr, conv_state, h0.reshape(bsz, 1, c), w_conv, b_conv.reshape(1, c), wax,
      b_a.reshape(1, c), b_x.reshape(1, c), sp)


def _layer_norm(z, g, b):
    mu = jnp.mean(z, axis=-1, keepdims=True)
    var = jnp.mean(jnp.square(z - mu), axis=-1, keepdims=True)
    return (z - mu) * lax.rsqrt(var + LN_EPS) * g + b


def _merge_kernel(attn_ref, rnn_ref, ga_ref, gb_ref, x_ref, wo_ref, g1_ref, b1_ref, wr_ref, br_ref,
                  x1_ref, x1b_ref, cw_ref, *, dn_alpha):
    merged = jax.nn.sigmoid(ga_ref[...]) * attn_ref[...] + jax.nn.sigmoid(gb_ref[...]) * rnn_ref[...]
    m = _dot(merged.astype(MXU_DTYPE), wo_ref[...])
    x1 = _layer_norm(dn_alpha * x_ref[...] + m, g1_ref[...], b1_ref[...])
    x1_ref[...] = x1
    x1b = x1.astype(MXU_DTYPE)
    x1b_ref[...] = x1b
    logits = _dot(x1b, wr_ref[...]) + br_ref[...]
    ne = logits.shape[1]
    eidx = lax.broadcasted_iota(i32, logits.shape, 1).astype(f32)
    work = logits
    chosen = jnp.zeros(logits.shape, jnp.bool_)
    top = None
    for kk in range(TOP_K):
        mx = jnp.max(work, axis=1, keepdims=True)
        if kk == 0:
            top = mx
        first = jnp.min(jnp.where(work == mx, eidx, ne), axis=1, keepdims=True)
        pick = eidx == first
        chosen = chosen | pick
        work = jnp.where(pick, NEG_INF, work)
    e = jnp.where(chosen, jnp.exp(logits - top), 0.0)
    cw_ref[...] = e / jnp.sum(e, axis=1, keepdims=True)


def _merge_ln_router(attn, rnn, ga, gb, x2d, w_out, ln_g, ln_b, w_router, b_router, *, tm, dn_alpha):
    n, d = x2d.shape
    ne = w_router.shape[1]
    row = lambda i: (i, 0)
    const = lambda i: (0, 0)
    kern = functools.partial(_merge_kernel, dn_alpha=dn_alpha)
    return pl.pallas_call(
        kern,
        grid=(n // tm,),
        in_specs=[pl.BlockSpec((tm, d), row)] * 5
                 + [pl.BlockSpec((d, d), const), pl.BlockSpec((1, d), const), pl.BlockSpec((1, d), const),
                    pl.BlockSpec((d, ne), const), pl.BlockSpec((1, ne), const)],
        out_specs=[pl.BlockSpec((tm, d), row), pl.BlockSpec((tm, d), row), pl.BlockSpec((tm, ne), row)],
        out_shape=[jax.ShapeDtypeStruct((n, d), f32), jax.ShapeDtypeStruct((n, d), MXU_DTYPE),
                   jax.ShapeDtypeStruct((n, ne), f32)],
        compiler_params=_cparams(("parallel",)),
        name="merge_ln_router",
    )(attn, rnn, ga, gb, x2d, w_out.astype(MXU_DTYPE), ln_g.reshape(1, d), ln_b.reshape(1, d),
      w_router.astype(MXU_DTYPE), b_router.reshape(1, ne))


MOE_CHUNK = 128


def _moe_kernel(xb_ref, cw_ref, wg_ref, bg_ref, wu_ref, bu_ref, wd_ref, bd_ref, o_ref,
                rank_scr, rankt_scr):
    e = pl.program_id(1)
    tm, ne = cw_ref.shape

    @pl.when(e == 0)
    def _():
        sel = jnp.where(cw_ref[...] > 0.0, 1.0, 0.0).astype(MXU_DTYPE)
        r = lax.broadcasted_iota(i32, (tm, tm), 0)
        c = lax.broadcasted_iota(i32, (tm, tm), 1)
        lower = jnp.where(c < r, 1.0, 0.0).astype(MXU_DTYPE)
        upper = jnp.where(r < c, 1.0, 0.0).astype(MXU_DTYPE)
        rank = _dot(lower, sel)
        rank_scr[...] = jnp.where(sel > 0, rank, -1.0)
        eye = jnp.where(lax.broadcasted_iota(i32, (ne, ne), 0) == lax.broadcasted_iota(i32, (ne, ne), 1),
                        1.0, 0.0).astype(MXU_DTYPE)
        selt = lax.dot_general(eye, sel, (((1,), (1,)), ((), ())), preferred_element_type=f32)
        rankt = _dot(selt.astype(MXU_DTYPE), upper)
        rankt_scr[...] = jnp.where(selt > 0, rankt, -1.0)
        o_ref[...] = jnp.zeros(o_ref.shape, f32)

    rrow = rankt_scr[pl.ds(e, 1), :]
    emask = lax.broadcasted_iota(i32, (tm, ne), 1) == e
    rcol = jnp.sum(jnp.where(emask, rank_scr[...], 0.0), axis=1, keepdims=True)
    gate = jnp.sum(jnp.where(emask, cw_ref[...], 0.0), axis=1, keepdims=True)
    cnt = jnp.max(rrow).astype(i32) + 1
    sub = lax.broadcasted_iota(i32, (MOE_CHUNK, tm), 0).astype(f32)
    lan = lax.broadcasted_iota(i32, (tm, MOE_CHUNK), 1).astype(f32)

    def chunk(cidx, carry):
        lo = (cidx * MOE_CHUNK).astype(f32)
        gather = jnp.where(rrow - lo == sub, 1.0, 0.0).astype(MXU_DTYPE)
        xg = _dot(gather, xb_ref[...]).astype(MXU_DTYPE)
        g = _dot(xg, wg_ref[...]) + bg_ref[...]
        u = _dot(xg, wu_ref[...]) + bu_ref[...]
        g = jnp.minimum(g, SWIGLU_LIMIT)
        u = jnp.clip(u, -SWIGLU_LIMIT, SWIGLU_LIMIT)
        hmid = g * jax.nn.sigmoid(SWIGLU_ALPHA * g) * (u + 1.0)
        y = _dot(hmid.astype(MXU_DTYPE), wd_ref[...]) + bd_ref[...]
        y_hi = y.astype(MXU_DTYPE)
        y_lo = (y - y_hi.astype(f32)).astype(MXU_DTYPE)
        scatter = jnp.where(rcol - lo == lan, 1.0, 0.0).astype(MXU_DTYPE)
        back = _dot(jnp.concatenate([scatter, scatter], axis=1), jnp.concatenate([y_hi, y_lo], axis=0))
        o_ref[...] += gate * back
        return carry

    lax.fori_loop(0, (cnt + MOE_CHUNK - 1) // MOE_CHUNK, chunk, 0)


def _moe(x1b, cw, w_gate, b_gate, w_up, b_up, w_down, b_down, *, tm):
    n, d = x1b.shape
    ne = cw.shape[1]
    dff = w_gate.shape[2]
    row = lambda i, e: (i, 0)
    wmap = lambda i, e: (e, 0, 0)
    return pl.pallas_call(
        _moe_kernel,
        grid=(n // tm, ne),
        in_specs=[pl.BlockSpec((tm, d), row), pl.BlockSpec((tm, ne), row),
                  pl.BlockSpec((None, d, dff), wmap), pl.BlockSpec((None, 1, dff), wmap),
                  pl.BlockSpec((None, d, dff), wmap), pl.BlockSpec((None, 1, dff), wmap),
                  pl.BlockSpec((None, dff, d), wmap), pl.BlockSpec((None, 1, d), wmap)],
        out_specs=pl.BlockSpec((tm, d), row),
        out_shape=jax.ShapeDtypeStruct((n, d), f32),
        scratch_shapes=[pltpu.VMEM((tm, ne), f32), pltpu.VMEM((ne, tm), f32)],
        compiler_params=_cparams(("parallel", "arbitrary")),
        name="moe_experts",
    )(x1b, cw, w_gate.astype(MXU_DTYPE), b_gate.reshape(ne, 1, dff), w_up.astype(MXU_DTYPE),
      b_up.reshape(ne, 1, dff), w_down.astype(MXU_DTYPE), b_down.reshape(ne, 1, d))


def _final_kernel(x1_ref, ffn_ref, p_ref, g2_ref, b2_ref, wpg_ref, wpp_ref, y_ref, *, dn_alpha):
    x2 = _layer_norm(dn_alpha * x1_ref[...] + ffn_ref[...], g2_ref[...], b2_ref[...])
    gate = jax.nn.sigmoid(_dot(x2.astype(MXU_DTYPE), wpg_ref[...]))
    emb = _dot(p_ref[...].astype(MXU_DTYPE), wpp_ref[...])
    y_ref[...] = x2 + gate * emb


def _final(x1, ffn, p2d, ln_g, ln_b, w_ple_gate, w_ple_proj, *, tm, dn_alpha):
    n, d = x1.shape
    dp = p2d.shape[1]
    row = lambda i: (i, 0)
    const = lambda i: (0, 0)
    kern = functools.partial(_final_kernel, dn_alpha=dn_alpha)
    return pl.pallas_call(
        kern,
        grid=(n // tm,),
        in_specs=[pl.BlockSpec((tm, d), row), pl.BlockSpec((tm, d), row), pl.BlockSpec((tm, dp), row),
                  pl.BlockSpec((1, d), const), pl.BlockSpec((1, d), const),
                  pl.BlockSpec((d, d), const), pl.BlockSpec((dp, d), const)],
        out_specs=pl.BlockSpec((tm, d), row),
        out_shape=jax.ShapeDtypeStruct((n, d), f32),
        compiler_params=_cparams(("parallel",)),
        name="ln2_ple",
    )(x1, ffn, p2d, ln_g.reshape(1, d), ln_b.reshape(1, d), w_ple_gate.astype(MXU_DTYPE),
      w_ple_proj.astype(MXU_DTYPE))


def _pick_tile(n, pref):
    t = min(pref, n)
    while n % t:
        t //= 2
    return t


def _layer(x, p, pos0, past_k, past_v, past_ki, conv_state, h0, prm, depth):
    (w_in, w_conv, b_conv, w_a, b_a, w_x, b_x, lam, w_out, ln1_g, ln1_b, w_router, b_router,
     w_gate, b_gate, w_up, b_up, w_down, b_down, ln2_g, ln2_b, w_ple_gate, w_ple_proj) = prm
    bsz, t, d = x.shape
    n = bsz * t
    past = past_k.shape[1]
    dn_alpha = (2 * depth) ** 0.25

    tm = _pick_tile(n, 512)
    pos = pos0 + jnp.arange(t)
    tabs_a = _rope_tables(pos, ROPE_DIM, HEAD_DIM)
    tabs_i = _rope_tables(pos, IDX_ROPE_DIM, IDX_DIM)
    if t % tm:
        rep = tm // t
        tabs_a = tuple(jnp.tile(a, (rep, 1)) for a in tabs_a)
        tabs_i = tuple(jnp.tile(a, (rep, 1)) for a in tabs_i)
    (q, k, kb, v, vb, qi, ki, kib, wi, xr, gr, ga, gb) = _in_projection(
        x.reshape(n, d), _pack_w_in(w_in), tabs_a, tabs_i, tm)

    s_true = past + t
    qb = min(Q_BLOCK, t)
    kblk = 1024 if s_true % 1024 == 0 else 3 * LANES
    s_pad = -(-s_true // kblk) * kblk

    def keys_by_position(past_arr, new_arr):
        new_arr = new_arr.reshape(bsz, t, -1)
        parts = [new_arr]
        if past:
            parts.insert(0, past_arr.reshape(bsz, past, new_arr.shape[2]).astype(MXU_DTYPE))
        if s_pad > s_true:
            parts.append(jnp.zeros((bsz, s_pad - s_true, new_arr.shape[2]), MXU_DTYPE))
        return jnp.concatenate(parts, axis=1)

    attn = _dsa_attention(q.reshape(bsz, t, -1), qi.reshape(bsz, t, -1), wi.reshape(bsz, t, -1),
                          keys_by_position(past_ki, kib), keys_by_position(past_k, kb),
                          keys_by_position(past_v, vb), qb=qb, kb=kblk, s_true=s_true, pos0=pos0)

    tt = _pick_tile(t, 512)
    xr3 = xr.reshape(bsz, t, -1)
    rnn, h_last = _rglru(xr3, gr.reshape(bsz, t, -1), conv_state, h0, w_conv, b_conv, w_a, b_a, w_x, b_x,
                         lam, tt=tt, reset_first=(pos0 == 0))
    new_conv = jnp.concatenate([conv_state, xr3], axis=1)[:, -(CONV_W - 1):]

    x1, x1b, cw = _merge_ln_router(attn.reshape(n, d), rnn.reshape(n, d), ga, gb, x.reshape(n, d), w_out,
                                   ln1_g, ln1_b, w_router, b_router, tm=tm, dn_alpha=dn_alpha)
    ffn = _moe(x1b, cw, w_gate, b_gate, w_up, b_up, w_down, b_down, tm=_pick_tile(n, 1024))
    y = _final(x1, ffn, p.reshape(n, -1), ln2_g, ln2_b, w_ple_gate, w_ple_proj, tm=tm, dn_alpha=dn_alpha)

    return (y.reshape(bsz, t, d), k.reshape(bsz, t, N_KV_HEADS, HEAD_DIM), v.reshape(bsz, t, N_KV_HEADS, HEAD_DIM),
            ki.reshape(bsz, t, IDX_DIM), new_conv, h_last.reshape(bsz, -1))


def kernel(x_prompt, x_sample, cache_k, cache_v, cache_kidx, state_conv, state_h, p_prompt, p_sample,
           w_in, w_conv, b_conv, w_a, b_a, w_x, b_x, lru_lambda, w_out, ln1_g, ln1_b, w_router, b_router,
           w_gate, b_gate, w_up, b_up, w_down, b_down, ln2_g, ln2_b, w_ple_gate, w_ple_proj):
    depth = w_in.shape[0]
    bp = x_prompt.shape[0]
    past_len = cache_k.shape[2]
    dt = x_prompt.dtype
    empty_kv = jnp.zeros((bp, 0, N_KV_HEADS, HEAD_DIM), dt)
    empty_ki = jnp.zeros((bp, 0, IDX_DIM), dt)
    zero_conv = jnp.zeros((bp, CONV_W - 1, x_prompt.shape[2]), dt)
    zero_h = jnp.zeros((bp, x_prompt.shape[2]), dt)

    yp, ys = x_prompt, x_sample
    outs_p, outs_s = [], []
    for l in range(depth):
        prm = (w_in[l], w_conv[l], b_conv[l], w_a[l], b_a[l], w_x[l], b_x[l], lru_lambda[l], w_out[l],
               ln1_g[l], ln1_b[l], w_router[l], b_router[l], w_gate[l], b_gate[l], w_up[l], b_up[l],
               w_down[l], b_down[l], ln2_g[l], ln2_b[l], w_ple_gate[l], w_ple_proj[l])
        yp, *rest_p = _layer(yp, p_prompt[l], 0, empty_kv, empty_kv, empty_ki, zero_conv, zero_h, prm, depth)
        ys, *rest_s = _layer(ys, p_sample[l], past_len, cache_k[l], cache_v[l], cache_kidx[l],
                             state_conv[l], state_h[l], prm, depth)
        outs_p.append(rest_p)
        outs_s.append(rest_s)

    stack = lambda outs, j: jnp.stack([o[j] for o in outs])
    return (yp, ys) + tuple(stack(outs_p, j) for j in range(5)) + tuple(stack(outs_s, j) for j in range(5))
```

```python
import functools

import jax
import jax.numpy as jnp
import numpy as np
from jax import lax
from jax.experimental import pallas as pl
from jax.experimental.pallas import tpu as pltpu

f32 = jnp.float32
i32 = jnp.int32
MXU_DTYPE = jnp.bfloat16

CHUNK = 64
HEAD_DIM = 128
N_KV_HEADS = 2
GROUP = 4
ROPE_DIM = 32
ROPE_THETA = 500000.0
N_IDX_HEADS = 8
IDX_DIM = 64
IDX_ROPE_DIM = 16
IDX_SCALE = (IDX_DIM * N_IDX_HEADS) ** -0.5
TOPK_KEYS = 256
Q_BLOCK = 128
N_RNN_BLOCKS = 16
RNN_BLOCK = 64
RNN_GROUP = 256
CONV_W = 4
LRU_C = 8.0
N_EXPERTS = 32
TOP_K = 4
SWIGLU_LIMIT = 7.0
SWIGLU_ALPHA = 1.702
LN_EPS = 1e-5

LANES = 128
SUBLANES = 8
VMEM_LIMIT = 56 * 1024 * 1024
INT_MIN = -2 ** 31
NEG_INF = float("-inf")
LOG2_E = 1.4426950408889634


def _cparams(sem):
    return pltpu.CompilerParams(dimension_semantics=sem, vmem_limit_bytes=VMEM_LIMIT)


def _resident(shape, index_map):
    return pl.BlockSpec(shape, index_map, pipeline_mode=pl.Buffered(1))


def _dot(a, b):
    return jnp.dot(a, b, preferred_element_type=f32)


_C_Q, _C_K, _C_V, _C_QI, _C_KW, _C_XR, _C_GR, _C_GA, _C_GB, _C_END = (
    0, 1024, 1280, 1536, 2048, 2176, 3200, 4224, 5248, 6272)


def _rope_tile(h, c, s_lo, s_hi, half):
    return h * c + pltpu.roll(h, half, 1) * s_hi + pltpu.roll(h, LANES - half, 1) * s_lo


def _inproj_kernel(x_ref, w_ref, ca_ref, sla_ref, sha_ref, ci_ref, sli_ref, shi_ref,
                   q_ref, k_ref, kb_ref, v_ref, vb_ref, qi_ref, ki_ref, kib_ref, wi_ref,
                   xr_ref, gr_ref, ga_ref, gb_ref):
    xb = x_ref[...].astype(MXU_DTYPE)

    def proj(a, b):
        return _dot(xb, w_ref[:, a:b])

    ca, sla, sha = ca_ref[...], sla_ref[...], sha_ref[...]
    ci, sli, shi = ci_ref[...], sli_ref[...], shi_ref[...]

    hq = proj(_C_Q, _C_K)
    for j in range(hq.shape[1] // LANES):
        t = _rope_tile(hq[:, j * LANES:(j + 1) * LANES], ca, sla, sha, ROPE_DIM // 2)
        q_ref[:, j * LANES:(j + 1) * LANES] = t.astype(q_ref.dtype)
    hk = proj(_C_K, _C_V)
    for j in range(hk.shape[1] // LANES):
        t = _rope_tile(hk[:, j * LANES:(j + 1) * LANES], ca, sla, sha, ROPE_DIM // 2)
        k_ref[:, j * LANES:(j + 1) * LANES] = t
        kb_ref[:, j * LANES:(j + 1) * LANES] = t.astype(kb_ref.dtype)
    hv = proj(_C_V, _C_QI)
    v_ref[...] = hv
    vb_ref[...] = hv.astype(vb_ref.dtype)
    hqi = proj(_C_QI, _C_KW)
    for j in range(hqi.shape[1] // LANES):
        t = _rope_tile(hqi[:, j * LANES:(j + 1) * LANES], ci, sli, shi, IDX_ROPE_DIM // 2)
        qi_ref[:, j * LANES:(j + 1) * LANES] = t.astype(qi_ref.dtype)
    hkw = proj(_C_KW, _C_XR)
    t = _rope_tile(hkw, ci, sli, shi, IDX_ROPE_DIM // 2)
    ki_ref[...] = t[:, :IDX_DIM]
    kib_ref[...] = t[:, :IDX_DIM].astype(kib_ref.dtype)
    wi_ref[...] = hkw[:, IDX_DIM:IDX_DIM + N_IDX_HEADS]
    xr_ref[...] = proj(_C_XR, _C_GR)
    gr_ref[...] = proj(_C_GR, _C_GA)
    ga_ref[...] = proj(_C_GA, _C_GB)
    gb_ref[...] = proj(_C_GB, _C_END)


def _rope_tables(pos, rot_dim, head_dim):
    half = rot_dim // 2
    inv = ROPE_THETA ** (-jnp.arange(half, dtype=f32) / half)
    ang = pos.astype(f32)[:, None] * inv[None, :]
    cos, sin = jnp.cos(ang), jnp.sin(ang)
    n = pos.shape[0]
    one = jnp.ones((n, head_dim - rot_dim), f32)
    zero_h = jnp.zeros((n, half), f32)
    zero_r = jnp.zeros((n, head_dim - rot_dim), f32)
    c = jnp.concatenate([cos, cos, one], axis=1)
    s_lo = jnp.concatenate([-sin, zero_h, zero_r], axis=1)
    s_hi = jnp.concatenate([zero_h, sin, zero_r], axis=1)
    rep = LANES // head_dim
    return tuple(jnp.tile(t, (1, rep)) for t in (c, s_lo, s_hi))


def _pack_w_in(w_in):
    d = w_in.shape[0]
    pad = jnp.zeros((d, LANES - IDX_DIM - N_IDX_HEADS), w_in.dtype)
    split = 1024 + 256 + 256 + 512 + IDX_DIM + N_IDX_HEADS
    return jnp.concatenate([w_in[:, :split], pad, w_in[:, split:]], axis=1).astype(MXU_DTYPE)


def _in_projection(x2d, w_packed, tabs_a, tabs_i, tm):
    n, d = x2d.shape
    nt = tabs_a[0].shape[0] // tm
    row = lambda i: (i, 0)
    tab = lambda i: (i % nt, 0)
    out_cols = [(1024, MXU_DTYPE), (256, f32), (256, MXU_DTYPE), (256, f32), (256, MXU_DTYPE),
                (512, MXU_DTYPE), (IDX_DIM, f32), (IDX_DIM, MXU_DTYPE), (N_IDX_HEADS, f32),
                (1024, f32), (1024, f32), (1024, f32), (1024, f32)]
    return pl.pallas_call(
        _inproj_kernel,
        grid=(n // tm,),
        in_specs=[pl.BlockSpec((tm, d), row), _resident(w_packed.shape, lambda i: (0, 0))]
                 + [pl.BlockSpec((tm, LANES), tab)] * 6,
        out_specs=[pl.BlockSpec((tm, c), row) for c, _ in out_cols],
        out_shape=[jax.ShapeDtypeStruct((n, c), dt) for c, dt in out_cols],
        compiler_params=_cparams(("parallel",)),
        name="in_projection",
    )(x2d, w_packed, *tabs_a, *tabs_i)


MASKED_LOGIT = -2.0 ** 126
MAX_FLOOR = -2.0 ** 120
UNKNOWN_COUNT = 1e9
MIN_NORMAL = 2.0 ** -126


def _dsa_kernel(q_ref, qi_ref, wi_ref, kit_ref, kt_ref, v_ref, o_ref,
                keys_scr, kaug_scr, sa_scr, sb_scr, vaug_scr, m_scr, acc_scr, upper_scr,
                *, qb, kb, s_true, pos0, n_sel):
    i = pl.program_id(1)

    @pl.when(i == 0)
    def _():
        upper_scr[...] = jnp.where(lax.broadcasted_iota(i32, (kb, kb), 0) < lax.broadcasted_iota(i32, (kb, kb), 1),
                                   1.0, 0.0).astype(MXU_DTYPE)
        ones_col = jnp.where(lax.broadcasted_iota(i32, (kb, HEAD_DIM), 1) == 0, 1.0, 0.0).astype(MXU_DTYPE)
        for g in range(N_KV_HEADS):
            vaug_scr[g, :, HEAD_DIM:] = ones_col

    row = lax.broadcasted_iota(i32, (qb, 1), 0)
    pos = pos0 + i * qb + row
    vis_end = jnp.minimum((pos // CHUNK + 1) * CHUNK, s_true)
    pos_last = pos0 + i * qb + (qb - 1)
    kend = jnp.minimum((pos_last // CHUNK + 1) * CHUNK, s_true)
    nkb = (kend + kb - 1) // kb
    lane = lax.broadcasted_iota(i32, (qb, kb), 1)

    qi = qi_ref[...]
    wi = wi_ref[...]

    def score_block(b, carry):
        kit = kit_ref[b]
        sc = jnp.zeros((qb, kb), f32)
        for h in range(N_IDX_HEADS):
            d = _dot(qi[:, h * IDX_DIM:(h + 1) * IDX_DIM], kit)
            sc = sc + wi[:, h:h + 1] * jnp.maximum(d, 0.0)
        sc = sc * IDX_SCALE
        sc = jnp.where(jnp.abs(sc) < MIN_NORMAL, 0.0, sc)
        bits = pltpu.bitcast(sc, i32)
        key = bits ^ ((bits >> 31) & 0x7FFFFFFF)
        keys_scr[b] = jnp.where(b * kb + lane < vis_end, key, INT_MIN)
        return carry

    lax.fori_loop(0, nkb, score_block, 0)

    def count(pred):
        def body(b, acc):
            for t in range(kb // LANES):
                hit = pred(keys_scr[b, :, t * LANES:(t + 1) * LANES], b * kb + t * LANES)
                acc = acc + jnp.where(hit, 1.0, 0.0)
            return acc
        acc = lax.fori_loop(0, nkb, body, jnp.zeros((qb, LANES), f32))
        return jnp.sum(acc, axis=1, keepdims=True)

    def wide(x):
        return jnp.broadcast_to(x, (qb, LANES))

    nsel = jnp.float32(n_sel)

    def unsettled(ct):
        return jnp.max(jnp.abs(ct - nsel)) > 0.0

    def bisect_cond(carry):
        t, tau, ct, hi = carry
        open_rows = jnp.where((ct == nsel) | (hi == tau + 1), 0.0, 1.0)
        return (t < 33) & (jnp.max(open_rows) > 0.0)

    def bisect_one(t, tau, ct, hi):
        shift = jnp.clip(jnp.where(t == 0, 31, 32 - t), 0, 31)
        cand = tau + jnp.where((t == 1) | (t > 32), 1, jnp.left_shift(jnp.int32(1), shift))
        cw = wide(cand)
        c = count(lambda k, c0: k >= cw)
        ok = c >= nsel
        return jnp.where(ok, cand, tau), jnp.where(ok, c, ct), jnp.where(ok, hi, cand)

    def bisect_step(carry):
        t, tau, ct, hi = carry
        tau, ct, hi = bisect_one(t, tau, ct, hi)
        tau, ct, hi = bisect_one(t + 1, tau, ct, hi)
        return t + 2, tau, ct, hi

    _, tau, ctau, _ = lax.while_loop(bisect_cond, bisect_step,
                                     (jnp.int32(0), jnp.full((qb, 1), INT_MIN, i32),
                                      jnp.full((qb, 1), UNKNOWN_COUNT, f32), jnp.full((qb, 1), INT_MIN, i32)))

    @pl.when(unsettled(ctau))
    def _():
        tw = wide(tau)
        need = nsel - count(lambda k, c0: k > tw)
        need = jnp.where(tau == INT_MIN, UNKNOWN_COUNT, need)

        def demote(b, seen):
            k = keys_scr[b]
            eq = k == tau
            before = _dot(jnp.where(eq, 1.0, 0.0).astype(MXU_DTYPE), upper_scr[...]) + seen
            keys_scr[b] = jnp.where(eq & (before >= need), tau - 1, k)
            return seen + jnp.sum(jnp.where(eq, 1.0, 0.0), axis=1, keepdims=True)

        lax.fori_loop(0, nkb, demote, jnp.zeros((qb, 1), f32))

    tau_sel = jnp.maximum(tau, INT_MIN + 1)

    q = q_ref[...]
    rq = lax.broadcasted_iota(i32, (GROUP * qb, qb), 0) % qb
    onehot = jnp.where(rq == lax.broadcasted_iota(i32, (GROUP * qb, qb), 1), 1.0, 0.0).astype(MXU_DTYPE)
    qaug = [jnp.concatenate([jnp.concatenate([q[:, (g * GROUP + j) * HEAD_DIM:(g * GROUP + j + 1) * HEAD_DIM]
                                              for j in range(GROUP)], axis=0), onehot], axis=1)
            for g in range(N_KV_HEADS)]
    m_scr[...] = jnp.full(m_scr.shape, MAX_FLOOR, f32)
    acc_scr[...] = jnp.zeros(acc_scr.shape, f32)
    c2 = HEAD_DIM ** -0.5 * LOG2_E

    def logits(b, s_ref):
        k = keys_scr[b]
        mask_rows = jnp.where(k >= tau_sel, 0.0, MASKED_LOGIT).astype(MXU_DTYPE)
        for g in range(N_KV_HEADS):
            kaug_scr[g, :HEAD_DIM, :] = kt_ref[b, g * HEAD_DIM:(g + 1) * HEAD_DIM, :]
            kaug_scr[g, HEAD_DIM:, :] = mask_rows
        for g in range(N_KV_HEADS):
            s_ref[g] = _dot(qaug[g], kaug_scr[g]) * c2

    def update(b, s_ref):
        for g in range(N_KV_HEADS):
            m_old = m_scr[g]
            m_new = jnp.maximum(m_old, jnp.max(s_ref[g], axis=1, keepdims=True))
            p = jnp.exp2(s_ref[g] - m_new)
            alpha = jnp.exp2(m_old - m_new)
            vaug_scr[g, :, :HEAD_DIM] = v_ref[b, :, g * HEAD_DIM:(g + 1) * HEAD_DIM]
            pv = _dot(p.astype(MXU_DTYPE), vaug_scr[g])
            acc_scr[g] = alpha * acc_scr[g] + pv
            m_scr[g] = m_new

    logits(0, sa_scr)

    def block_pair(ip, carry):
        b0 = 2 * ip
        logits(jnp.minimum(b0 + 1, nkb - 1), sb_scr)
        update(b0, sa_scr)

        @pl.when(b0 + 1 < nkb)
        def _():
            logits(jnp.minimum(b0 + 2, nkb - 1), sa_scr)
            update(b0 + 1, sb_scr)
        return carry

    lax.fori_loop(0, (nkb + 1) // 2, block_pair, 0)

    for g in range(N_KV_HEADS):
        o = acc_scr[g, :, :HEAD_DIM] / acc_scr[g, :, HEAD_DIM:HEAD_DIM + 1]
        for j in range(GROUP):
            h = g * GROUP + j
            o_ref[:, h * HEAD_DIM:(h + 1) * HEAD_DIM] = o[j * qb:(j + 1) * qb, :]


def _dsa_attention(q, qi, wi, ki_all, k_all, v_all, *, qb, kb, s_true, pos0):
    bsz, t, _ = q.shape
    s_pad = k_all.shape[1]
    nk = s_pad // kb
    n_sel = min(TOPK_KEYS, s_true // 4)
    kit = ki_all.reshape(bsz, nk, kb, IDX_DIM).transpose(0, 1, 3, 2)
    kt = k_all.reshape(bsz, nk, kb, N_KV_HEADS * HEAD_DIM).transpose(0, 1, 3, 2)
    vv = v_all.reshape(bsz, nk, kb, N_KV_HEADS * HEAD_DIM)
    qrow = lambda b, i: (b, i, 0)
    kmap = lambda b, i: (b, 0, 0, 0)
    kern = functools.partial(_dsa_kernel, qb=qb, kb=kb, s_true=s_true, pos0=pos0, n_sel=n_sel)
    return pl.pallas_call(
        kern,
        grid=(bsz, t // qb),
        in_specs=[pl.BlockSpec((None, qb, q.shape[2]), qrow),
                  pl.BlockSpec((None, qb, qi.shape[2]), qrow),
                  pl.BlockSpec((None, qb, wi.shape[2]), qrow),
                  _resident((None, nk, IDX_DIM, kb), kmap),
                  _resident((None, nk, N_KV_HEADS * HEAD_DIM, kb), kmap),
                  _resident((None, nk, kb, N_KV_HEADS * HEAD_DIM), kmap)],
        out_specs=pl.BlockSpec((None, qb, q.shape[2]), qrow),
        out_shape=jax.ShapeDtypeStruct(q.shape, f32),
        scratch_shapes=[pltpu.VMEM((nk, qb, kb), i32),
                        pltpu.VMEM((N_KV_HEADS, HEAD_DIM + qb, kb), MXU_DTYPE),
                        pltpu.VMEM((N_KV_HEADS, GROUP * qb, kb), f32),
                        pltpu.VMEM((N_KV_HEADS, GROUP * qb, kb), f32),
                        pltpu.VMEM((N_KV_HEADS, kb, 2 * HEAD_DIM), MXU_DTYPE),
                        pltpu.VMEM((N_KV_HEADS, GROUP * qb, 1), f32),
                        pltpu.VMEM((N_KV_HEADS, GROUP * qb, 2 * HEAD_DIM), f32),
                        pltpu.VMEM((kb, kb), MXU_DTYPE)],
        compiler_params=_cparams(("parallel", "arbitrary")),
        name="dsa_attention",
    )(q, qi, wi, kit, kt, vv)


def _expm1(y):
    u = jnp.exp(y)
    um1 = u - 1.0
    return jnp.where(um1 == 0.0, y, jnp.where(um1 == -1.0, -1.0, um1 * y / jnp.log(u)))


def _rglru_kernel(xr_ref, gr_ref, cs_ref, h0_ref, wc_ref, bc_ref, wax_ref, ba_ref, bx_ref, lam_ref,
                  rnn_ref, hlast_ref, xp_scr, a_scr, b_scr, h_scr, *, reset_first):
    t = pl.program_id(1)
    tt, c = xr_ref.shape
    lead = SUBLANES

    @pl.when(t == 0)
    def _():
        xp_scr[lead - (CONV_W - 1):lead, :] = cs_ref[...]
        h_scr[...] = h0_ref[...]

    xp_scr[lead:lead + tt, :] = xr_ref[...]
    conv = bc_ref[...]
    for j in range(CONV_W):
        off = lead - (CONV_W - 1) + j
        conv = conv + xp_scr[off:off + tt, :] * wc_ref[j:j + 1, :]
    xp_scr[lead - (CONV_W - 1):lead, :] = xp_scr[lead + tt - (CONV_W - 1):lead + tt, :]

    cb = conv.astype(MXU_DTYPE)
    sp = lam_ref[...]
    grow = t * tt + lax.broadcasted_iota(i32, (tt, 1), 0)
    for g in range(c // RNN_GROUP):
        sl = slice(g * RNN_GROUP, (g + 1) * RNN_GROUP)
        z = _dot(cb[:, sl], wax_ref[g])
        r = jax.nn.sigmoid(z[:, :RNN_GROUP] + ba_ref[:, sl])
        ig = jax.nn.sigmoid(z[:, RNN_GROUP:] + bx_ref[:, sl])
        log_a = -LRU_C * r * sp[:, sl]
        mult = jnp.sqrt(-_expm1(2.0 * log_a))
        if reset_first:
            mult = jnp.where(grow == 0, 1.0, mult)
        a_scr[:, sl] = jnp.exp(log_a)
        b_scr[:, sl] = mult * (ig * conv[:, sl])

    def rows(jb, h):
        base = pl.multiple_of(jb * SUBLANES, SUBLANES)
        for u in range(SUBLANES):
            h = a_scr[pl.ds(base + u, 1), :] * h + b_scr[pl.ds(base + u, 1), :]
            rnn_ref[pl.ds(base + u, 1), :] = h
        return h

    h = lax.fori_loop(0, tt // SUBLANES, rows, h_scr[...])
    h_scr[...] = h
    rnn_ref[...] = rnn_ref[...] * jax.nn.gelu(gr_ref[...])

    @pl.when(t == pl.num_programs(1) - 1)
    def _():
        hlast_ref[...] = h


def _blockdiag_groups(w):
    per = RNN_GROUP // RNN_BLOCK
    g = w.reshape(N_RNN_BLOCKS // per, per, RNN_BLOCK, RNN_BLOCK)
    eye = jnp.eye(per, dtype=w.dtype)
    return jnp.einsum('gacd,ab->gacbd', g, eye).reshape(N_RNN_BLOCKS // per, RNN_GROUP, RNN_GROUP)


def _rglru(xr, gr, conv_state, h0, w_conv, b_conv, w_a, b_a, w_x, b_x, lam, *, tt, reset_first):
    bsz, t, c = xr.shape
    wax = jnp.concatenate([_blockdiag_groups(w_a), _blockdiag_groups(w_x)], axis=2).astype(MXU_DTYPE)
    sp = jax.nn.softplus(-lam.astype(f32)).reshape(1, c)
    seq = lambda b, i: (b, i, 0)
    per_b = lambda b, i: (b, 0, 0)
    const2 = lambda b, i: (0, 0)
    const3 = lambda b, i: (0, 0, 0)
    kern = functools.partial(_rglru_kernel, reset_first=reset_first)
    return pl.pallas_call(
        kern,
        grid=(bsz, t // tt),
        in_specs=[pl.BlockSpec((None, tt, c), seq), pl.BlockSpec((None, tt, c), seq),
                  pl.BlockSpec((None, CONV_W - 1, c), per_b), pl.BlockSpec((None, 1, c), per_b),
                  pl.BlockSpec((CONV_W, c), const2), pl.BlockSpec((1, c), const2),
                  pl.BlockSpec(wax.shape, const3), pl.BlockSpec((1, c), const2),
                  pl.BlockSpec((1, c), const2), pl.BlockSpec((1, c), const2)],
        out_specs=[pl.BlockSpec((None, tt, c), seq), pl.BlockSpec((None, 1, c), per_b)],
        out_shape=[jax.ShapeDtypeStruct((bsz, t, c), f32), jax.ShapeDtypeStruct((bsz, 1, c), f32)],
        scratch_shapes=[pltpu.VMEM((tt + SUBLANES, c), f32), pltpu.VMEM((tt, c), f32),
                        pltpu.VMEM((tt, c), f32), pltpu.VMEM((1, c), f32)],
        compiler_params=_cparams(("parallel", "arbitrary")),
        name="rglru",
    )(xr, gr, conv_state, h0.reshape(bsz, 1, c), w_conv, b_conv.reshape(1, c), wax,
      b_a.reshape(1, c), b_x.reshape(1, c), sp)


def _layer_norm(z, g, b):
    mu = jnp.mean(z, axis=-1, keepdims=True)
    var = jnp.mean(jnp.square(z - mu), axis=-1, keepdims=True)
    return (z - mu) * lax.rsqrt(var + LN_EPS) * g + b


def _merge_kernel(attn_ref, rnn_ref, ga_ref, gb_ref, x_ref, wo_ref, g1_ref, b1_ref, wr_ref, br_ref,
                  x1_ref, x1b_ref, cw_ref, *, dn_alpha):
    merged = jax.nn.sigmoid(ga_ref[...]) * attn_ref[...] + jax.nn.sigmoid(gb_ref[...]) * rnn_ref[...]
    m = _dot(merged.astype(MXU_DTYPE), wo_ref[...])
    x1 = _layer_norm(dn_alpha * x_ref[...] + m, g1_ref[...], b1_ref[...])
    x1_ref[...] = x1
    x1b = x1.astype(MXU_DTYPE)
    x1b_ref[...] = x1b
    logits = _dot(x1b, wr_ref[...]) + br_ref[...]
    ne = logits.shape[1]
    eidx = lax.broadcasted_iota(i32, logits.shape, 1).astype(f32)
    work = logits
    chosen = jnp.zeros(logits.shape, jnp.bool_)
    top = None
    for kk in range(TOP_K):
        mx = jnp.max(work, axis=1, keepdims=True)
        if kk == 0:
            top = mx
        first = jnp.min(jnp.where(work == mx, eidx, ne), axis=1, keepdims=True)
        pick = eidx == first
        chosen = chosen | pick
        work = jnp.where(pick, NEG_INF, work)
    e = jnp.where(chosen, jnp.exp(logits - top), 0.0)
    cw_ref[...] = e / jnp.sum(e, axis=1, keepdims=True)


def _merge_ln_router(attn, rnn, ga, gb, x2d, w_out, ln_g, ln_b, w_router, b_router, *, tm, dn_alpha):
    n, d = x2d.shape
    ne = w_router.shape[1]
    row = lambda i: (i, 0)
    const = lambda i: (0, 0)
    kern = functools.partial(_merge_kernel, dn_alpha=dn_alpha)
    return pl.pallas_call(
        kern,
        grid=(n // tm,),
        in_specs=[pl.BlockSpec((tm, d), row)] * 5
                 + [pl.BlockSpec((d, d), const), pl.BlockSpec((1, d), const), pl.BlockSpec((1, d), const),
                    pl.BlockSpec((d, ne), const), pl.BlockSpec((1, ne), const)],
        out_specs=[pl.BlockSpec((tm, d), row), pl.BlockSpec((tm, d), row), pl.BlockSpec((tm, ne), row)],
        out_shape=[jax.ShapeDtypeStruct((n, d), f32), jax.ShapeDtypeStruct((n, d), MXU_DTYPE),
                   jax.ShapeDtypeStruct((n, ne), f32)],
        compiler_params=_cparams(("parallel",)),
        name="merge_ln_router",
    )(attn, rnn, ga, gb, x2d, w_out.astype(MXU_DTYPE), ln_g.reshape(1, d), ln_b.reshape(1, d),
      w_router.astype(MXU_DTYPE), b_router.reshape(1, ne))


MOE_CHUNK = 128


def _moe_kernel(xb_ref, cwt_ref, wg_ref, bg_ref, wu_ref, bu_ref, wd_ref, bd_ref, o_ref, rankt_scr):
    e = pl.program_id(1)
    ne, tm = cwt_ref.shape

    @pl.when(e == 0)
    def _():
        selt = jnp.where(cwt_ref[...] > 0.0, 1.0, 0.0)
        upper = jnp.where(lax.broadcasted_iota(i32, (tm, tm), 0) < lax.broadcasted_iota(i32, (tm, tm), 1),
                          1.0, 0.0).astype(MXU_DTYPE)
        rankt = _dot(selt.astype(MXU_DTYPE), upper)
        rankt_scr[...] = jnp.where(selt > 0.0, rankt, -1.0)
        o_ref[...] = jnp.zeros(o_ref.shape, f32)

    rrow = rankt_scr[pl.ds(e, 1), :]
    grow = cwt_ref[pl.ds(e, 1), :]
    cnt = jnp.max(rrow).astype(i32) + 1

    def run_chunk(first_rank, size):
        lo = first_rank.astype(f32)
        hit = rrow - lo == lax.broadcasted_iota(i32, (size, tm), 0).astype(f32)
        gather = jnp.where(hit, 1.0, 0.0).astype(MXU_DTYPE)
        gcomp = jnp.sum(jnp.where(hit, grow, 0.0), axis=1, keepdims=True)
        xg = _dot(gather, xb_ref[...]).astype(MXU_DTYPE)
        g = _dot(xg, wg_ref[...]) + bg_ref[...]
        u = _dot(xg, wu_ref[...]) + bu_ref[...]
        g = jnp.minimum(g, SWIGLU_LIMIT)
        u = jnp.clip(u, -SWIGLU_LIMIT, SWIGLU_LIMIT)
        hmid = g * jax.nn.sigmoid(SWIGLU_ALPHA * g) * (u + 1.0)
        y = (_dot(hmid.astype(MXU_DTYPE), wd_ref[...]) + bd_ref[...]) * gcomp
        y_hi = y.astype(MXU_DTYPE)
        y_lo = (y - y_hi.astype(f32)).astype(MXU_DTYPE)
        o_ref[...] += lax.dot_general(jnp.concatenate([gather, gather], axis=0),
                                      jnp.concatenate([y_hi, y_lo], axis=0),
                                      (((0,), (0,)), ((), ())), preferred_element_type=f32)

    def chunk(cidx, carry):
        run_chunk(cidx * MOE_CHUNK, MOE_CHUNK)
        return carry

    lax.fori_loop(0, (cnt + MOE_CHUNK - 1) // MOE_CHUNK, chunk, 0)


def _moe(x1b, cw, w_gate, b_gate, w_up, b_up, w_down, b_down, *, tm):
    n, d = x1b.shape
    ne = cw.shape[1]
    dff = w_gate.shape[2]
    row = lambda i, e: (i, 0)
    wmap = lambda i, e: (e, 0, 0)
    return pl.pallas_call(
        _moe_kernel,
        grid=(n // tm, ne),
        in_specs=[pl.BlockSpec((tm, d), row), pl.BlockSpec((ne, tm), lambda i, e: (0, i)),
                  pl.BlockSpec((None, d, dff), wmap), pl.BlockSpec((None, 1, dff), wmap),
                  pl.BlockSpec((None, d, dff), wmap), pl.BlockSpec((None, 1, dff), wmap),
                  pl.BlockSpec((None, dff, d), wmap), pl.BlockSpec((None, 1, d), wmap)],
        out_specs=pl.BlockSpec((tm, d), row),
        out_shape=jax.ShapeDtypeStruct((n, d), f32),
        scratch_shapes=[pltpu.VMEM((ne, tm), f32)],
        compiler_params=_cparams(("parallel", "arbitrary")),
        name="moe_experts",
    )(x1b, cw.T, w_gate.astype(MXU_DTYPE), b_gate.reshape(ne, 1, dff), w_up.astype(MXU_DTYPE),
      b_up.reshape(ne, 1, dff), w_down.astype(MXU_DTYPE), b_down.reshape(ne, 1, d))


def _final_kernel(x1_ref, ffn_ref, p_ref, g2_ref, b2_ref, wpg_ref, wpp_ref, y_ref, *, dn_alpha):
    x2 = _layer_norm(dn_alpha * x1_ref[...] + ffn_ref[...], g2_ref[...], b2_ref[...])
    gate = jax.nn.sigmoid(_dot(x2.astype(MXU_DTYPE), wpg_ref[...]))
    emb = _dot(p_ref[...].astype(MXU_DTYPE), wpp_ref[...])
    y_ref[...] = x2 + gate * emb


def _final(x1, ffn, p2d, ln_g, ln_b, w_ple_gate, w_ple_proj, *, tm, dn_alpha):
    n, d = x1.shape
    dp = p2d.shape[1]
    row = lambda i: (i, 0)
    const = lambda i: (0, 0)
    kern = functools.partial(_final_kernel, dn_alpha=dn_alpha)
    return pl.pallas_call(
        kern,
        grid=(n // tm,),
        in_specs=[pl.BlockSpec((tm, d), row), pl.BlockSpec((tm, d), row), pl.BlockSpec((tm, dp), row),
                  pl.BlockSpec((1, d), const), pl.BlockSpec((1, d), const),
                  pl.BlockSpec((d, d), const), pl.BlockSpec((dp, d), const)],
        out_specs=pl.BlockSpec((tm, d), row),
        out_shape=jax.ShapeDtypeStruct((n, d), f32),
        compiler_params=_cparams(("parallel",)),
        name="ln2_ple",
    )(x1, ffn, p2d, ln_g.reshape(1, d), ln_b.reshape(1, d), w_ple_gate.astype(MXU_DTYPE),
      w_ple_proj.astype(MXU_DTYPE))


def _pick_tile(n, pref):
    t = min(pref, n)
    while n % t:
        t //= 2
    return t


def _layer(x, p, pos0, past_k, past_v, past_ki, conv_state, h0, prm, depth):
    (w_in, w_conv, b_conv, w_a, b_a, w_x, b_x, lam, w_out, ln1_g, ln1_b, w_router, b_router,
     w_gate, b_gate, w_up, b_up, w_down, b_down, ln2_g, ln2_b, w_ple_gate, w_ple_proj) = prm
    bsz, t, d = x.shape
    n = bsz * t
    past = past_k.shape[1]
    dn_alpha = (2 * depth) ** 0.25

    tm = _pick_tile(n, 512)
    pos = pos0 + jnp.arange(t)
    tabs_a = _rope_tables(pos, ROPE_DIM, HEAD_DIM)
    tabs_i = _rope_tables(pos, IDX_ROPE_DIM, IDX_DIM)
    if t % tm:
        rep = tm // t
        tabs_a = tuple(jnp.tile(a, (rep, 1)) for a in tabs_a)
        tabs_i = tuple(jnp.tile(a, (rep, 1)) for a in tabs_i)
    (q, k, kb, v, vb, qi, ki, kib, wi, xr, gr, ga, gb) = _in_projection(
        x.reshape(n, d), _pack_w_in(w_in), tabs_a, tabs_i, tm)

    s_true = past + t
    qb = min(Q_BLOCK, t)
    kblk = 1024 if s_true % 1024 == 0 else 3 * LANES
    s_pad = -(-s_true // kblk) * kblk

    def keys_by_position(past_arr, new_arr):
        new_arr = new_arr.reshape(bsz, t, -1)
        parts = [new_arr]
        if past:
            parts.insert(0, past_arr.reshape(bsz, past, new_arr.shape[2]).astype(MXU_DTYPE))
        if s_pad > s_true:
            parts.append(jnp.zeros((bsz, s_pad - s_true, new_arr.shape[2]), MXU_DTYPE))
        return jnp.concatenate(parts, axis=1)

    attn = _dsa_attention(q.reshape(bsz, t, -1), qi.reshape(bsz, t, -1), wi.reshape(bsz, t, -1),
                          keys_by_position(past_ki, kib), keys_by_position(past_k, kb),
                          keys_by_position(past_v, vb), qb=qb, kb=kblk, s_true=s_true, pos0=pos0)

    tt = _pick_tile(t, 512)
    xr3 = xr.reshape(bsz, t, -1)
    rnn, h_last = _rglru(xr3, gr.reshape(bsz, t, -1), conv_state, h0, w_conv, b_conv, w_a, b_a, w_x, b_x,
                         lam, tt=tt, reset_first=(pos0 == 0))
    new_conv = jnp.concatenate([conv_state, xr3], axis=1)[:, -(CONV_W - 1):]

    x1, x1b, cw = _merge_ln_router(attn.reshape(n, d), rnn.reshape(n, d), ga, gb, x.reshape(n, d), w_out,
                                   ln1_g, ln1_b, w_router, b_router, tm=tm, dn_alpha=dn_alpha)
    ffn = _moe(x1b, cw, w_gate, b_gate, w_up, b_up, w_down, b_down, tm=_pick_tile(n, 1024))
    y = _final(x1, ffn, p.reshape(n, -1), ln2_g, ln2_b, w_ple_gate, w_ple_proj, tm=tm, dn_alpha=dn_alpha)

    return (y.reshape(bsz, t, d), k.reshape(bsz, t, N_KV_HEADS, HEAD_DIM), v.reshape(bsz, t, N_KV_HEADS, HEAD_DIM),
            ki.reshape(bsz, t, IDX_DIM), new_conv, h_last.reshape(bsz, -1))


def kernel(x_prompt, x_sample, cache_k, cache_v, cache_kidx, state_conv, state_h, p_prompt, p_sample,
           w_in, w_conv, b_conv, w_a, b_a, w_x, b_x, lru_lambda, w_out, ln1_g, ln1_b, w_router, b_router,
           w_gate, b_gate, w_up, b_up, w_down, b_down, ln2_g, ln2_b, w_ple_gate, w_ple_proj):
    depth = w_in.shape[0]
    bp = x_prompt.shape[0]
    past_len = cache_k.shape[2]
    dt = x_prompt.dtype
    empty_kv = jnp.zeros((bp, 0, N_KV_HEADS, HEAD_DIM), dt)
    empty_ki = jnp.zeros((bp, 0, IDX_DIM), dt)
    zero_conv = jnp.zeros((bp, CONV_W - 1, x_prompt.shape[2]), dt)
    zero_h = jnp.zeros((bp, x_prompt.shape[2]), dt)

    yp, ys = x_prompt, x_sample
    outs_p, outs_s = [], []
    for l in range(depth):
        prm = (w_in[l], w_conv[l], b_conv[l], w_a[l], b_a[l], w_x[l], b_x[l], lru_lambda[l], w_out[l],
               ln1_g[l], ln1_b[l], w_router[l], b_router[l], w_gate[l], b_gate[l], w_up[l], b_up[l],
               w_down[l], b_down[l], ln2_g[l], ln2_b[l], w_ple_gate[l], w_ple_proj[l])
        yp, *rest_p = _layer(yp, p_prompt[l], 0, empty_kv, empty_kv, empty_ki, zero_conv, zero_h, prm, depth)
        ys, *rest_s = _layer(ys, p_sample[l], past_len, cache_k[l], cache_v[l], cache_kidx[l],
                             state_conv[l], state_h[l], prm, depth)
        outs_p.append(rest_p)
        outs_s.append(rest_s)

    stack = lambda outs, j: jnp.stack([o[j] for o in outs])
    return (yp, ys) + tuple(stack(outs_p, j) for j in range(5)) + tuple(stack(outs_s, j) for j in range(5))
```

```python
import functools

import jax
import jax.numpy as jnp
import numpy as np
from jax import lax
from jax.experimental import pallas as pl
from jax.experimental.pallas import tpu as pltpu

f32 = jnp.float32
i32 = jnp.int32
MXU_DTYPE = jnp.bfloat16

CHUNK = 64
HEAD_DIM = 128
N_KV_HEADS = 2
GROUP = 4
ROPE_DIM = 32
ROPE_THETA = 500000.0
N_IDX_HEADS = 8
IDX_DIM = 64
IDX_ROPE_DIM = 16
IDX_SCALE = (IDX_DIM * N_IDX_HEADS) ** -0.5
TOPK_KEYS = 256
Q_BLOCK = 128
N_RNN_BLOCKS = 16
RNN_BLOCK = 64
RNN_GROUP = 256
CONV_W = 4
LRU_C = 8.0
N_EXPERTS = 32
TOP_K = 4
SWIGLU_LIMIT = 7.0
SWIGLU_ALPHA = 1.702
LN_EPS = 1e-5

LANES = 128
SUBLANES = 8
VMEM_LIMIT = 56 * 1024 * 1024
INT_MIN = -2 ** 31
NEG_INF = float("-inf")
LOG2_E = 1.4426950408889634


def _cparams(sem):
    return pltpu.CompilerParams(dimension_semantics=sem, vmem_limit_bytes=VMEM_LIMIT)


def _resident(shape, index_map):
    return pl.BlockSpec(shape, index_map, pipeline_mode=pl.Buffered(1))


def _dot(a, b):
    return jnp.dot(a, b, preferred_element_type=f32)


_C_Q, _C_K, _C_V, _C_QI, _C_KW, _C_XR, _C_GR, _C_GA, _C_GB, _C_END = (
    0, 1024, 1280, 1536, 2048, 2176, 3200, 4224, 5248, 6272)


def _rope_tile(h, c, s_lo, s_hi, half):
    return h * c + pltpu.roll(h, half, 1) * s_hi + pltpu.roll(h, LANES - half, 1) * s_lo


def _inproj_kernel(x_ref, w_ref, ca_ref, sla_ref, sha_ref, ci_ref, sli_ref, shi_ref,
                   q_ref, k_ref, kb_ref, v_ref, vb_ref, qi_ref, ki_ref, kib_ref, wi_ref,
                   xr_ref, gr_ref, ga_ref, gb_ref):
    xb = x_ref[...].astype(MXU_DTYPE)

    def proj(a, b):
        return _dot(xb, w_ref[:, a:b])

    ca, sla, sha = ca_ref[...], sla_ref[...], sha_ref[...]
    ci, sli, shi = ci_ref[...], sli_ref[...], shi_ref[...]

    hq = proj(_C_Q, _C_K)
    for j in range(hq.shape[1] // LANES):
        t = _rope_tile(hq[:, j * LANES:(j + 1) * LANES], ca, sla, sha, ROPE_DIM // 2)
        q_ref[:, j * LANES:(j + 1) * LANES] = t.astype(q_ref.dtype)
    hk = proj(_C_K, _C_V)
    for j in range(hk.shape[1] // LANES):
        t = _rope_tile(hk[:, j * LANES:(j + 1) * LANES], ca, sla, sha, ROPE_DIM // 2)
        k_ref[:, j * LANES:(j + 1) * LANES] = t
        kb_ref[:, j * LANES:(j + 1) * LANES] = t.astype(kb_ref.dtype)
    hv = proj(_C_V, _C_QI)
    v_ref[...] = hv
    vb_ref[...] = hv.astype(vb_ref.dtype)
    hqi = proj(_C_QI, _C_KW)
    for j in range(hqi.shape[1] // LANES):
        t = _rope_tile(hqi[:, j * LANES:(j + 1) * LANES], ci, sli, shi, IDX_ROPE_DIM // 2)
        qi_ref[:, j * LANES:(j + 1) * LANES] = t.astype(qi_ref.dtype)
    hkw = proj(_C_KW, _C_XR)
    t = _rope_tile(hkw, ci, sli, shi, IDX_ROPE_DIM // 2)
    ki_ref[...] = t[:, :IDX_DIM]
    kib_ref[...] = t[:, :IDX_DIM].astype(kib_ref.dtype)
    wi_ref[...] = hkw[:, IDX_DIM:IDX_DIM + N_IDX_HEADS]
    xr_ref[...] = proj(_C_XR, _C_GR)
    gr_ref[...] = proj(_C_GR, _C_GA)
    ga_ref[...] = proj(_C_GA, _C_GB)
    gb_ref[...] = proj(_C_GB, _C_END)


def _rope_tables(pos, rot_dim, head_dim):
    half = rot_dim // 2
    inv = ROPE_THETA ** (-jnp.arange(half, dtype=f32) / half)
    ang = pos.astype(f32)[:, None] * inv[None, :]
    cos, sin = jnp.cos(ang), jnp.sin(ang)
    n = pos.shape[0]
    one = jnp.ones((n, head_dim - rot_dim), f32)
    zero_h = jnp.zeros((n, half), f32)
    zero_r = jnp.zeros((n, head_dim - rot_dim), f32)
    c = jnp.concatenate([cos, cos, one], axis=1)
    s_lo = jnp.concatenate([-sin, zero_h, zero_r], axis=1)
    s_hi = jnp.concatenate([zero_h, sin, zero_r], axis=1)
    rep = LANES // head_dim
    return tuple(jnp.tile(t, (1, rep)) for t in (c, s_lo, s_hi))


def _pack_w_in(w_in):
    d = w_in.shape[0]
    pad = jnp.zeros((d, LANES - IDX_DIM - N_IDX_HEADS), w_in.dtype)
    split = 1024 + 256 + 256 + 512 + IDX_DIM + N_IDX_HEADS
    return jnp.concatenate([w_in[:, :split], pad, w_in[:, split:]], axis=1).astype(MXU_DTYPE)


def _in_projection(x2d, w_packed, tabs_a, tabs_i, tm):
    n, d = x2d.shape
    nt = tabs_a[0].shape[0] // tm
    row = lambda i: (i, 0)
    tab = lambda i: (i % nt, 0)
    out_cols = [(1024, MXU_DTYPE), (256, f32), (256, MXU_DTYPE), (256, f32), (256, MXU_DTYPE),
                (512, MXU_DTYPE), (IDX_DIM, f32), (IDX_DIM, MXU_DTYPE), (N_IDX_HEADS, f32),
                (1024, f32), (1024, f32), (1024, f32), (1024, f32)]
    return pl.pallas_call(
        _inproj_kernel,
        grid=(n // tm,),
        in_specs=[pl.BlockSpec((tm, d), row), _resident(w_packed.shape, lambda i: (0, 0))]
                 + [pl.BlockSpec((tm, LANES), tab)] * 6,
        out_specs=[pl.BlockSpec((tm, c), row) for c, _ in out_cols],
        out_shape=[jax.ShapeDtypeStruct((n, c), dt) for c, dt in out_cols],
        compiler_params=_cparams(("parallel",)),
        name="in_projection",
    )(x2d, w_packed, *tabs_a, *tabs_i)


MASKED_LOGIT = -2.0 ** 126
MAX_FLOOR = -2.0 ** 120
UNKNOWN_COUNT = 1e9
MIN_NORMAL = 2.0 ** -126


SEARCH_FALLBACK = 40


def _score_key(s):
    bits = pltpu.bitcast(s, i32)
    return bits ^ ((bits >> 31) & 0x7FFFFFFF)


def _key_score(k):
    return pltpu.bitcast(k ^ ((k >> 31) & 0x7FFFFFFF), f32)


def _dsa_kernel(q_ref, qi_ref, wi_ref, kit_ref, kt_ref, v_ref, o_ref,
                keys_scr, kaug_scr, sa_scr, sb_scr, vaug_scr, m_scr, acc_scr, upper_scr,
                *, qb, kb, s_true, pos0, n_sel):
    i = pl.program_id(1)

    @pl.when(i == 0)
    def _():
        upper_scr[...] = jnp.where(lax.broadcasted_iota(i32, (kb, kb), 0) < lax.broadcasted_iota(i32, (kb, kb), 1),
                                   1.0, 0.0).astype(MXU_DTYPE)
        ones_col = jnp.where(lax.broadcasted_iota(i32, (kb, HEAD_DIM), 1) == 0, 1.0, 0.0).astype(MXU_DTYPE)
        for g in range(N_KV_HEADS):
            vaug_scr[g, :, HEAD_DIM:] = ones_col

    row = lax.broadcasted_iota(i32, (qb, 1), 0)
    pos = pos0 + i * qb + row
    vis_end = jnp.minimum((pos // CHUNK + 1) * CHUNK, s_true)
    pos_last = pos0 + i * qb + (qb - 1)
    kend = jnp.minimum((pos_last // CHUNK + 1) * CHUNK, s_true)
    nkb = (kend + kb - 1) // kb
    lane = lax.broadcasted_iota(i32, (qb, kb), 1)

    qi = qi_ref[...]
    wi = wi_ref[...]

    def score_block(b, carry):
        kit = kit_ref[b]
        sc = jnp.zeros((qb, kb), f32)
        for h in range(N_IDX_HEADS):
            d = _dot(qi[:, h * IDX_DIM:(h + 1) * IDX_DIM], kit)
            sc = sc + wi[:, h:h + 1] * jnp.maximum(d, 0.0)
        sc = sc * IDX_SCALE
        sc = jnp.where(jnp.abs(sc) < MIN_NORMAL, 0.0, sc)
        adm = b * kb + lane < vis_end
        keys_scr[b] = jnp.where(adm, _score_key(sc), INT_MIN)
        smax, smin = carry
        hi_part = jnp.where(adm, sc, NEG_INF)
        lo_part = jnp.where(adm, sc, -NEG_INF)
        for t in range(kb // LANES):
            smax = jnp.maximum(smax, hi_part[:, t * LANES:(t + 1) * LANES])
            smin = jnp.minimum(smin, lo_part[:, t * LANES:(t + 1) * LANES])
        return smax, smin

    smax, smin = lax.fori_loop(0, nkb, score_block,
                               (jnp.full((qb, LANES), NEG_INF, f32), jnp.full((qb, LANES), -NEG_INF, f32)))
    smax = jnp.max(smax, axis=1, keepdims=True)
    smin = jnp.min(smin, axis=1, keepdims=True)

    def count(pred):
        def body(b, acc):
            for t in range(kb // LANES):
                hit = pred(keys_scr[b, :, t * LANES:(t + 1) * LANES], b * kb + t * LANES)
                acc = acc + jnp.where(hit, 1.0, 0.0)
            return acc
        acc = lax.fori_loop(0, nkb, body, jnp.zeros((qb, LANES), f32))
        return jnp.sum(acc, axis=1, keepdims=True)

    def wide(x):
        return jnp.broadcast_to(x, (qb, LANES))

    nsel = jnp.float32(n_sel)

    def unsettled(ct):
        return jnp.max(jnp.abs(ct - nsel)) > 0.0

    def search_cond(carry):
        t, lo, hi, clo = carry[:4]
        open_rows = jnp.where((clo == nsel) | (hi == lo + 1), 0.0, 1.0)
        return jnp.max(open_rows) > 0.0

    def search_one(t, lo, hi, clo, glo, ghi, side):
        vlo, vhi = _key_score(lo), _key_score(hi - 1)
        frac = glo / (glo - ghi)
        cand = _score_key(vlo + (vhi - vlo) * frac)
        cand = jnp.where(t >= SEARCH_FALLBACK, lo + jnp.right_shift(hi - lo, 1), cand)
        cand = jnp.where((t == 0) & (lo < 0) & (hi > 0), 0, cand)
        cand = jnp.where((t == 1) & (lo <= 0) & (hi > 1), 1, cand)
        cand = jnp.maximum(lo + 1, jnp.minimum(cand, hi - 1))
        cw = wide(cand)
        c = count(lambda k, c0: k >= cw)
        ok = c >= nsel
        g = jnp.log(jnp.maximum(c, 0.5) / nsel)
        ghi = jnp.where(ok & (side > 0), 0.5 * ghi, ghi)
        glo = jnp.where(jnp.logical_not(ok) & (side < 0), 0.5 * glo, glo)
        return (jnp.where(ok, cand, lo), jnp.where(ok, hi, cand), jnp.where(ok, c, clo),
                jnp.where(ok, g, glo), jnp.where(ok, ghi, g), jnp.where(ok, 1, -1))

    def search_step(carry):
        t, state = carry[0], carry[1:]
        state = search_one(t, *state)
        state = search_one(t + 1, *state)
        return (t + 2,) + state

    nvis = vis_end.astype(f32)
    short = nvis <= nsel
    lo0 = jnp.where(short, INT_MIN, _score_key(smin))
    hi0 = jnp.where(short, INT_MIN + 1, _score_key(smax) + 1)
    clo0 = jnp.where(short, UNKNOWN_COUNT, nvis)
    glo0 = jnp.log(jnp.maximum(nvis, nsel) / nsel)
    ghi0 = jnp.full((qb, 1), np.log(0.5 / n_sel), f32)
    _, tau, _, ctau, _, _, _ = lax.while_loop(
        search_cond, search_step, (jnp.int32(0), lo0, hi0, clo0, glo0, ghi0, jnp.zeros((qb, 1), i32)))

    @pl.when(unsettled(ctau))
    def _():
        tw = wide(tau)
        need = nsel - count(lambda k, c0: k > tw)
        need = jnp.where(tau == INT_MIN, UNKNOWN_COUNT, need)

        def demote(b, seen):
            k = keys_scr[b]
            eq = k == tau
            before = _dot(jnp.where(eq, 1.0, 0.0).astype(MXU_DTYPE), upper_scr[...]) + seen
            keys_scr[b] = jnp.where(eq & (before >= need), tau - 1, k)
            return seen + jnp.sum(jnp.where(eq, 1.0, 0.0), axis=1, keepdims=True)

        lax.fori_loop(0, nkb, demote, jnp.zeros((qb, 1), f32))

    tau_sel = jnp.maximum(tau, INT_MIN + 1)

    q = q_ref[...]
    rq = lax.broadcasted_iota(i32, (GROUP * qb, qb), 0) % qb
    onehot = jnp.where(rq == lax.broadcasted_iota(i32, (GROUP * qb, qb), 1), 1.0, 0.0).astype(MXU_DTYPE)
    qaug = [jnp.concatenate([jnp.concatenate([q[:, (g * GROUP + j) * HEAD_DIM:(g * GROUP + j + 1) * HEAD_DIM]
                                              for j in range(GROUP)], axis=0), onehot], axis=1)
            for g in range(N_KV_HEADS)]
    m_scr[...] = jnp.full(m_scr.shape, MAX_FLOOR, f32)
    acc_scr[...] = jnp.zeros(acc_scr.shape, f32)
    c2 = HEAD_DIM ** -0.5 * LOG2_E

    def logits(b, s_ref):
        k = keys_scr[b]
        mask_rows = jnp.where(k >= tau_sel, 0.0, MASKED_LOGIT).astype(MXU_DTYPE)
        for g in range(N_KV_HEADS):
            kaug_scr[g, :HEAD_DIM, :] = kt_ref[b, g * HEAD_DIM:(g + 1) * HEAD_DIM, :]
            kaug_scr[g, HEAD_DIM:, :] = mask_rows
        for g in range(N_KV_HEADS):
            s_ref[g] = _dot(qaug[g], kaug_scr[g]) * c2

    def update(b, s_ref):
        for g in range(N_KV_HEADS):
            m_old = m_scr[g]
            m_new = jnp.maximum(m_old, jnp.max(s_ref[g], axis=1, keepdims=True))
            p = jnp.exp2(s_ref[g] - m_new)
            alpha = jnp.exp2(m_old - m_new)
            vaug_scr[g, :, :HEAD_DIM] = v_ref[b, :, g * HEAD_DIM:(g + 1) * HEAD_DIM]
            pv = _dot(p.astype(MXU_DTYPE), vaug_scr[g])
            acc_scr[g] = alpha * acc_scr[g] + pv
            m_scr[g] = m_new

    logits(0, sa_scr)

    def block_pair(ip, carry):
        b0 = 2 * ip
        logits(jnp.minimum(b0 + 1, nkb - 1), sb_scr)
        update(b0, sa_scr)

        @pl.when(b0 + 1 < nkb)
        def _():
            logits(jnp.minimum(b0 + 2, nkb - 1), sa_scr)
            update(b0 + 1, sb_scr)
        return carry

    lax.fori_loop(0, (nkb + 1) // 2, block_pair, 0)

    for g in range(N_KV_HEADS):
        o = acc_scr[g, :, :HEAD_DIM] / acc_scr[g, :, HEAD_DIM:HEAD_DIM + 1]
        for j in range(GROUP):
            h = g * GROUP + j
            o_ref[:, h * HEAD_DIM:(h + 1) * HEAD_DIM] = o[j * qb:(j + 1) * qb, :]


def _dsa_attention(q, qi, wi, ki_all, k_all, v_all, *, qb, kb, s_true, pos0):
    bsz, t, _ = q.shape
    s_pad = k_all.shape[1]
    nk = s_pad // kb
    n_sel = min(TOPK_KEYS, s_true // 4)
    kit = ki_all.reshape(bsz, nk, kb, IDX_DIM).transpose(0, 1, 3, 2)
    kt = k_all.reshape(bsz, nk, kb, N_KV_HEADS * HEAD_DIM).transpose(0, 1, 3, 2)
    vv = v_all.reshape(bsz, nk, kb, N_KV_HEADS * HEAD_DIM)
    qrow = lambda b, i: (b, i, 0)
    kmap = lambda b, i: (b, 0, 0, 0)
    kern = functools.partial(_dsa_kernel, qb=qb, kb=kb, s_true=s_true, pos0=pos0, n_sel=n_sel)
    return pl.pallas_call(
        kern,
        grid=(bsz, t // qb),
        in_specs=[pl.BlockSpec((None, qb, q.shape[2]), qrow),
                  pl.BlockSpec((None, qb, qi.shape[2]), qrow),
                  pl.BlockSpec((None, qb, wi.shape[2]), qrow),
                  _resident((None, nk, IDX_DIM, kb), kmap),
                  _resident((None, nk, N_KV_HEADS * HEAD_DIM, kb), kmap),
                  _resident((None, nk, kb, N_KV_HEADS * HEAD_DIM), kmap)],
        out_specs=pl.BlockSpec((None, qb, q.shape[2]), qrow),
        out_shape=jax.ShapeDtypeStruct(q.shape, f32),
        scratch_shapes=[pltpu.VMEM((nk, qb, kb), i32),
                        pltpu.VMEM((N_KV_HEADS, HEAD_DIM + qb, kb), MXU_DTYPE),
                        pltpu.VMEM((N_KV_HEADS, GROUP * qb, kb), f32),
                        pltpu.VMEM((N_KV_HEADS, GROUP * qb, kb), f32),
                        pltpu.VMEM((N_KV_HEADS, kb, 2 * HEAD_DIM), MXU_DTYPE),
                        pltpu.VMEM((N_KV_HEADS, GROUP * qb, 1), f32),
                        pltpu.VMEM((N_KV_HEADS, GROUP * qb, 2 * HEAD_DIM), f32),
                        pltpu.VMEM((kb, kb), MXU_DTYPE)],
        compiler_params=_cparams(("parallel", "arbitrary")),
        name="dsa_attention",
    )(q, qi, wi, kit, kt, vv)


def _expm1(y):
    u = jnp.exp(y)
    um1 = u - 1.0
    return jnp.where(um1 == 0.0, y, jnp.where(um1 == -1.0, -1.0, um1 * y / jnp.log(u)))


def _rglru_kernel(xr_ref, gr_ref, cs_ref, h0_ref, wc_ref, bc_ref, wax_ref, ba_ref, bx_ref, lam_ref,
                  rnn_ref, hlast_ref, xp_scr, a_scr, b_scr, h_scr, *, reset_first):
    t = pl.program_id(1)
    tt, c = xr_ref.shape
    lead = SUBLANES

    @pl.when(t == 0)
    def _():
        xp_scr[lead - (CONV_W - 1):lead, :] = cs_ref[...]
        h_scr[...] = h0_ref[...]

    xp_scr[lead:lead + tt, :] = xr_ref[...]
    conv = bc_ref[...]
    for j in range(CONV_W):
        off = lead - (CONV_W - 1) + j
        conv = conv + xp_scr[off:off + tt, :] * wc_ref[j:j + 1, :]
    xp_scr[lead - (CONV_W - 1):lead, :] = xp_scr[lead + tt - (CONV_W - 1):lead + tt, :]

    cb = conv.astype(MXU_DTYPE)
    sp = lam_ref[...]
    grow = t * tt + lax.broadcasted_iota(i32, (tt, 1), 0)
    for g in range(c // RNN_GROUP):
        sl = slice(g * RNN_GROUP, (g + 1) * RNN_GROUP)
        z = _dot(cb[:, sl], wax_ref[g])
        r = jax.nn.sigmoid(z[:, :RNN_GROUP] + ba_ref[:, sl])
        ig = jax.nn.sigmoid(z[:, RNN_GROUP:] + bx_ref[:, sl])
        log_a = -LRU_C * r * sp[:, sl]
        mult = jnp.sqrt(-_expm1(2.0 * log_a))
        if reset_first:
            mult = jnp.where(grow == 0, 1.0, mult)
        a_scr[:, sl] = jnp.exp(log_a)
        b_scr[:, sl] = mult * (ig * conv[:, sl])

    def rows(jb, h):
        base = pl.multiple_of(jb * SUBLANES, SUBLANES)
        for u in range(SUBLANES):
            h = a_scr[pl.ds(base + u, 1), :] * h + b_scr[pl.ds(base + u, 1), :]
            rnn_ref[pl.ds(base + u, 1), :] = h
        return h

    h = lax.fori_loop(0, tt // SUBLANES, rows, h_scr[...])
    h_scr[...] = h
    rnn_ref[...] = rnn_ref[...] * jax.nn.gelu(gr_ref[...])

    @pl.when(t == pl.num_programs(1) - 1)
    def _():
        hlast_ref[...] = h


def _blockdiag_groups(w):
    per = RNN_GROUP // RNN_BLOCK
    g = w.reshape(N_RNN_BLOCKS // per, per, RNN_BLOCK, RNN_BLOCK)
    eye = jnp.eye(per, dtype=w.dtype)
    return jnp.einsum('gacd,ab->gacbd', g, eye).reshape(N_RNN_BLOCKS // per, RNN_GROUP, RNN_GROUP)


def _rglru(xr, gr, conv_state, h0, w_conv, b_conv, w_a, b_a, w_x, b_x, lam, *, tt, reset_first):
    bsz, t, c = xr.shape
    wax = jnp.concatenate([_blockdiag_groups(w_a), _blockdiag_groups(w_x)], axis=2).astype(MXU_DTYPE)
    sp = jax.nn.softplus(-lam.astype(f32)).reshape(1, c)
    seq = lambda b, i: (b, i, 0)
    per_b = lambda b, i: (b, 0, 0)
    const2 = lambda b, i: (0, 0)
    const3 = lambda b, i: (0, 0, 0)
    kern = functools.partial(_rglru_kernel, reset_first=reset_first)
    return pl.pallas_call(
        kern,
        grid=(bsz, t // tt),
        in_specs=[pl.BlockSpec((None, tt, c), seq), pl.BlockSpec((None, tt, c), seq),
                  pl.BlockSpec((None, CONV_W - 1, c), per_b), pl.BlockSpec((None, 1, c), per_b),
                  pl.BlockSpec((CONV_W, c), const2), pl.BlockSpec((1, c), const2),
                  pl.BlockSpec(wax.shape, const3), pl.BlockSpec((1, c), const2),
                  pl.BlockSpec((1, c), const2), pl.BlockSpec((1, c), const2)],
        out_specs=[pl.BlockSpec((None, tt, c), seq), pl.BlockSpec((None, 1, c), per_b)],
        out_shape=[jax.ShapeDtypeStruct((bsz, t, c), f32), jax.ShapeDtypeStruct((bsz, 1, c), f32)],
        scratch_shapes=[pltpu.VMEM((tt + SUBLANES, c), f32), pltpu.VMEM((tt, c), f32),
                        pltpu.VMEM((tt, c), f32), pltpu.VMEM((1, c), f32)],
        compiler_params=_cparams(("parallel", "arbitrary")),
        name="rglru",
    )(xr, gr, conv_state, h0.reshape(bsz, 1, c), w_conv, b_conv.reshape(1, c), wax,
      b_a.reshape(1, c), b_x.reshape(1, c), sp)


def _layer_norm(z, g, b):
    mu = jnp.mean(z, axis=-1, keepdims=True)
    var = jnp.mean(jnp.square(z - mu), axis=-1, keepdims=True)
    return (z - mu) * lax.rsqrt(var + LN_EPS) * g + b


def _merge_kernel(attn_ref, rnn_ref, ga_ref, gb_ref, x_ref, wo_ref, g1_ref, b1_ref, wr_ref, br_ref,
                  x1_ref, x1b_ref, cw_ref, *, dn_alpha):
    merged = jax.nn.sigmoid(ga_ref[...]) * attn_ref[...] + jax.nn.sigmoid(gb_ref[...]) * rnn_ref[...]
    m = _dot(merged.astype(MXU_DTYPE), wo_ref[...])
    x1 = _layer_norm(dn_alpha * x_ref[...] + m, g1_ref[...], b1_ref[...])
    x1_ref[...] = x1
    x1b = x1.astype(MXU_DTYPE)
    x1b_ref[...] = x1b
    logits = _dot(x1b, wr_ref[...]) + br_ref[...]
    ne = logits.shape[1]
    eidx = lax.broadcasted_iota(i32, logits.shape, 1).astype(f32)
    work = logits
    chosen = jnp.zeros(logits.shape, jnp.bool_)
    top = None
    for kk in range(TOP_K):
        mx = jnp.max(work, axis=1, keepdims=True)
        if kk == 0:
            top = mx
        first = jnp.min(jnp.where(work == mx, eidx, ne), axis=1, keepdims=True)
        pick = eidx == first
        chosen = chosen | pick
        work = jnp.where(pick, NEG_INF, work)
    e = jnp.where(chosen, jnp.exp(logits - top), 0.0)
    cw_ref[...] = e / jnp.sum(e, axis=1, keepdims=True)


def _merge_ln_router(attn, rnn, ga, gb, x2d, w_out, ln_g, ln_b, w_router, b_router, *, tm, dn_alpha):
    n, d = x2d.shape
    ne = w_router.shape[1]
    row = lambda i: (i, 0)
    const = lambda i: (0, 0)
    kern = functools.partial(_merge_kernel, dn_alpha=dn_alpha)
    return pl.pallas_call(
        kern,
        grid=(n // tm,),
        in_specs=[pl.BlockSpec((tm, d), row)] * 5
                 + [pl.BlockSpec((d, d), const), pl.BlockSpec((1, d), const), pl.BlockSpec((1, d), const),
                    pl.BlockSpec((d, ne), const), pl.BlockSpec((1, ne), const)],
        out_specs=[pl.BlockSpec((tm, d), row), pl.BlockSpec((tm, d), row), pl.BlockSpec((tm, ne), row)],
        out_shape=[jax.ShapeDtypeStruct((n, d), f32), jax.ShapeDtypeStruct((n, d), MXU_DTYPE),
                   jax.ShapeDtypeStruct((n, ne), f32)],
        compiler_params=_cparams(("parallel",)),
        name="merge_ln_router",
    )(attn, rnn, ga, gb, x2d, w_out.astype(MXU_DTYPE), ln_g.reshape(1, d), ln_b.reshape(1, d),
      w_router.astype(MXU_DTYPE), b_router.reshape(1, ne))


MOE_CHUNK = 128


def _moe_kernel(xb_ref, cwt_ref, wg_ref, bg_ref, wu_ref, bu_ref, wd_ref, bd_ref, o_ref, rankt_scr):
    e = pl.program_id(1)
    ne, tm = cwt_ref.shape

    @pl.when(e == 0)
    def _():
        selt = jnp.where(cwt_ref[...] > 0.0, 1.0, 0.0)
        upper = jnp.where(lax.broadcasted_iota(i32, (tm, tm), 0) < lax.broadcasted_iota(i32, (tm, tm), 1),
                          1.0, 0.0).astype(MXU_DTYPE)
        rankt = _dot(selt.astype(MXU_DTYPE), upper)
        rankt_scr[...] = jnp.where(selt > 0.0, rankt, -1.0)
        o_ref[...] = jnp.zeros(o_ref.shape, f32)

    rrow = rankt_scr[pl.ds(e, 1), :]
    grow = cwt_ref[pl.ds(e, 1), :]
    cnt = jnp.max(rrow).astype(i32) + 1

    def run_chunk(first_rank, size):
        lo = first_rank.astype(f32)
        hit = rrow - lo == lax.broadcasted_iota(i32, (size, tm), 0).astype(f32)
        gather = jnp.where(hit, 1.0, 0.0).astype(MXU_DTYPE)
        gcomp = jnp.sum(jnp.where(hit, grow, 0.0), axis=1, keepdims=True)
        xg = _dot(gather, xb_ref[...]).astype(MXU_DTYPE)
        g = _dot(xg, wg_ref[...]) + bg_ref[...]
        u = _dot(xg, wu_ref[...]) + bu_ref[...]
        g = jnp.minimum(g, SWIGLU_LIMIT)
        u = jnp.clip(u, -SWIGLU_LIMIT, SWIGLU_LIMIT)
        hmid = g * jax.nn.sigmoid(SWIGLU_ALPHA * g) * (u + 1.0)
        y = (_dot(hmid.astype(MXU_DTYPE), wd_ref[...]) + bd_ref[...]) * gcomp
        y_hi = y.astype(MXU_DTYPE)
        y_lo = (y - y_hi.astype(f32)).astype(MXU_DTYPE)
        o_ref[...] += lax.dot_general(jnp.concatenate([gather, gather], axis=0),
                                      jnp.concatenate([y_hi, y_lo], axis=0),
                                      (((0,), (0,)), ((), ())), preferred_element_type=f32)

    def chunk(cidx, carry):
        run_chunk(cidx * MOE_CHUNK, MOE_CHUNK)
        return carry

    lax.fori_loop(0, (cnt + MOE_CHUNK - 1) // MOE_CHUNK, chunk, 0)


def _moe(x1b, cw, w_gate, b_gate, w_up, b_up, w_down, b_down, *, tm):
    n, d = x1b.shape
    ne = cw.shape[1]
    dff = w_gate.shape[2]
    row = lambda i, e: (i, 0)
    wmap = lambda i, e: (e, 0, 0)
    return pl.pallas_call(
        _moe_kernel,
        grid=(n // tm, ne),
        in_specs=[pl.BlockSpec((tm, d), row), pl.BlockSpec((ne, tm), lambda i, e: (0, i)),
                  pl.BlockSpec((None, d, dff), wmap), pl.BlockSpec((None, 1, dff), wmap),
                  pl.BlockSpec((None, d, dff), wmap), pl.BlockSpec((None, 1, dff), wmap),
                  pl.BlockSpec((None, dff, d), wmap), pl.BlockSpec((None, 1, d), wmap)],
        out_specs=pl.BlockSpec((tm, d), row),
        out_shape=jax.ShapeDtypeStruct((n, d), f32),
        scratch_shapes=[pltpu.VMEM((ne, tm), f32)],
        compiler_params=_cparams(("parallel", "arbitrary")),
        name="moe_experts",
    )(x1b, cw.T, w_gate.astype(MXU_DTYPE), b_gate.reshape(ne, 1, dff), w_up.astype(MXU_DTYPE),
      b_up.reshape(ne, 1, dff), w_down.astype(MXU_DTYPE), b_down.reshape(ne, 1, d))


def _final_kernel(x1_ref, ffn_ref, p_ref, g2_ref, b2_ref, wpg_ref, wpp_ref, y_ref, *, dn_alpha):
    x2 = _layer_norm(dn_alpha * x1_ref[...] + ffn_ref[...], g2_ref[...], b2_ref[...])
    gate = jax.nn.sigmoid(_dot(x2.astype(MXU_DTYPE), wpg_ref[...]))
    emb = _dot(p_ref[...].astype(MXU_DTYPE), wpp_ref[...])
    y_ref[...] = x2 + gate * emb


def _final(x1, ffn, p2d, ln_g, ln_b, w_ple_gate, w_ple_proj, *, tm, dn_alpha):
    n, d = x1.shape
    dp = p2d.shape[1]
    row = lambda i: (i, 0)
    const = lambda i: (0, 0)
    kern = functools.partial(_final_kernel, dn_alpha=dn_alpha)
    return pl.pallas_call(
        kern,
        grid=(n // tm,),
        in_specs=[pl.BlockSpec((tm, d), row), pl.BlockSpec((tm, d), row), pl.BlockSpec((tm, dp), row),
                  pl.BlockSpec((1, d), const), pl.BlockSpec((1, d), const),
                  pl.BlockSpec((d, d), const), pl.BlockSpec((dp, d), const)],
        out_specs=pl.BlockSpec((tm, d), row),
        out_shape=jax.ShapeDtypeStruct((n, d), f32),
        compiler_params=_cparams(("parallel",)),
        name="ln2_ple",
    )(x1, ffn, p2d, ln_g.reshape(1, d), ln_b.reshape(1, d), w_ple_gate.astype(MXU_DTYPE),
      w_ple_proj.astype(MXU_DTYPE))


def _pick_tile(n, pref):
    t = min(pref, n)
    while n % t:
        t //= 2
    return t


def _layer(x, p, pos0, past_k, past_v, past_ki, conv_state, h0, prm, depth):
    (w_in, w_conv, b_conv, w_a, b_a, w_x, b_x, lam, w_out, ln1_g, ln1_b, w_router, b_router,
     w_gate, b_gate, w_up, b_up, w_down, b_down, ln2_g, ln2_b, w_ple_gate, w_ple_proj) = prm
    bsz, t, d = x.shape
    n = bsz * t
    past = past_k.shape[1]
    dn_alpha = (2 * depth) ** 0.25

    tm = _pick_tile(n, 512)
    pos = pos0 + jnp.arange(t)
    tabs_a = _rope_tables(pos, ROPE_DIM, HEAD_DIM)
    tabs_i = _rope_tables(pos, IDX_ROPE_DIM, IDX_DIM)
    if t % tm:
        rep = tm // t
        tabs_a = tuple(jnp.tile(a, (rep, 1)) for a in tabs_a)
        tabs_i = tuple(jnp.tile(a, (rep, 1)) for a in tabs_i)
    (q, k, kb, v, vb, qi, ki, kib, wi, xr, gr, ga, gb) = _in_projection(
        x.reshape(n, d), _pack_w_in(w_in), tabs_a, tabs_i, tm)

    s_true = past + t
    qb = min(Q_BLOCK, t)
    kblk = 1024 if s_true % 1024 == 0 else 3 * LANES
    s_pad = -(-s_true // kblk) * kblk

    def keys_by_position(past_arr, new_arr):
        new_arr = new_arr.reshape(bsz, t, -1)
        parts = [new_arr]
        if past:
            parts.insert(0, past_arr.reshape(bsz, past, new_arr.shape[2]).astype(MXU_DTYPE))
        if s_pad > s_true:
            parts.append(jnp.zeros((bsz, s_pad - s_true, new_arr.shape[2]), MXU_DTYPE))
        return jnp.concatenate(parts, axis=1)

    attn = _dsa_attention(q.reshape(bsz, t, -1), qi.reshape(bsz, t, -1), wi.reshape(bsz, t, -1),
                          keys_by_position(past_ki, kib), keys_by_position(past_k, kb),
                          keys_by_position(past_v, vb), qb=qb, kb=kblk, s_true=s_true, pos0=pos0)

    tt = _pick_tile(t, 512)
    xr3 = xr.reshape(bsz, t, -1)
    rnn, h_last = _rglru(xr3, gr.reshape(bsz, t, -1), conv_state, h0, w_conv, b_conv, w_a, b_a, w_x, b_x,
                         lam, tt=tt, reset_first=(pos0 == 0))
    new_conv = jnp.concatenate([conv_state, xr3], axis=1)[:, -(CONV_W - 1):]

    x1, x1b, cw = _merge_ln_router(attn.reshape(n, d), rnn.reshape(n, d), ga, gb, x.reshape(n, d), w_out,
                                   ln1_g, ln1_b, w_router, b_router, tm=tm, dn_alpha=dn_alpha)
    ffn = _moe(x1b, cw, w_gate, b_gate, w_up, b_up, w_down, b_down, tm=_pick_tile(n, 1024))
    y = _final(x1, ffn, p.reshape(n, -1), ln2_g, ln2_b, w_ple_gate, w_ple_proj, tm=tm, dn_alpha=dn_alpha)

    return (y.reshape(bsz, t, d), k.reshape(bsz, t, N_KV_HEADS, HEAD_DIM), v.reshape(bsz, t, N_KV_HEADS, HEAD_DIM),
            ki.reshape(bsz, t, IDX_DIM), new_conv, h_last.reshape(bsz, -1))


def kernel(x_prompt, x_sample, cache_k, cache_v, cache_kidx, state_conv, state_h, p_prompt, p_sample,
           w_in, w_conv, b_conv, w_a, b_a, w_x, b_x, lru_lambda, w_out, ln1_g, ln1_b, w_router, b_router,
           w_gate, b_gate, w_up, b_up, w_down, b_down, ln2_g, ln2_b, w_ple_gate, w_ple_proj):
    depth = w_in.shape[0]
    bp = x_prompt.shape[0]
    past_len = cache_k.shape[2]
    dt = x_prompt.dtype
    empty_kv = jnp.zeros((bp, 0, N_KV_HEADS, HEAD_DIM), dt)
    empty_ki = jnp.zeros((bp, 0, IDX_DIM), dt)
    zero_conv = jnp.zeros((bp, CONV_W - 1, x_prompt.shape[2]), dt)
    zero_h = jnp.zeros((bp, x_prompt.shape[2]), dt)

    yp, ys = x_prompt, x_sample
    outs_p, outs_s = [], []
    for l in range(depth):
        prm = (w_in[l], w_conv[l], b_conv[l], w_a[l], b_a[l], w_x[l], b_x[l], lru_lambda[l], w_out[l],
               ln1_g[l], ln1_b[l], w_router[l], b_router[l], w_gate[l], b_gate[l], w_up[l], b_up[l],
               w_down[l], b_down[l], ln2_g[l], ln2_b[l], w_ple_gate[l], w_ple_proj[l])
        yp, *rest_p = _layer(yp, p_prompt[l], 0, empty_kv, empty_kv, empty_ki, zero_conv, zero_h, prm, depth)
        ys, *rest_s = _layer(ys, p_sample[l], past_len, cache_k[l], cache_v[l], cache_kidx[l],
                             state_conv[l], state_h[l], prm, depth)
        outs_p.append(rest_p)
        outs_s.append(rest_s)

    stack = lambda outs, j: jnp.stack([o[j] for o in outs])
    return (yp, ys) + tuple(stack(outs_p, j) for j in range(5)) + tuple(stack(outs_s, j) for j in range(5))
```

```python
import functools

import jax
import jax.numpy as jnp
import numpy as np
from jax import lax
from jax.experimental import pallas as pl
from jax.experimental.pallas import tpu as pltpu

f32 = jnp.float32
i32 = jnp.int32
MXU_DTYPE = jnp.bfloat16

CHUNK = 64
HEAD_DIM = 128
N_KV_HEADS = 2
GROUP = 4
ROPE_DIM = 32
ROPE_THETA = 500000.0
N_IDX_HEADS = 8
IDX_DIM = 64
IDX_ROPE_DIM = 16
IDX_SCALE = (IDX_DIM * N_IDX_HEADS) ** -0.5
TOPK_KEYS = 256
Q_BLOCK = 128
N_RNN_BLOCKS = 16
RNN_BLOCK = 64
RNN_GROUP = 256
CONV_W = 4
LRU_C = 8.0
N_EXPERTS = 32
TOP_K = 4
SWIGLU_LIMIT = 7.0
SWIGLU_ALPHA = 1.702
LN_EPS = 1e-5

LANES = 128
SUBLANES = 8
VMEM_LIMIT = 56 * 1024 * 1024
INT_MIN = -2 ** 31
NEG_INF = float("-inf")
LOG2_E = 1.4426950408889634


def _cparams(sem):
    return pltpu.CompilerParams(dimension_semantics=sem, vmem_limit_bytes=VMEM_LIMIT)


def _resident(shape, index_map):
    return pl.BlockSpec(shape, index_map, pipeline_mode=pl.Buffered(1))


def _dot(a, b):
    return jnp.dot(a, b, preferred_element_type=f32)


_C_Q, _C_K, _C_V, _C_QI, _C_KW, _C_XR, _C_GR, _C_GA, _C_GB, _C_END = (
    0, 1024, 1280, 1536, 2048, 2176, 3200, 4224, 5248, 6272)


def _rope_tile(h, c, s_lo, s_hi, half):
    return h * c + pltpu.roll(h, half, 1) * s_hi + pltpu.roll(h, LANES - half, 1) * s_lo


def _inproj_kernel(x_ref, w_ref, ca_ref, sla_ref, sha_ref, ci_ref, sli_ref, shi_ref,
                   q_ref, k_ref, kb_ref, v_ref, vb_ref, qi_ref, ki_ref, kib_ref, wi_ref,
                   xr_ref, gr_ref, ga_ref, gb_ref):
    xb = x_ref[...].astype(MXU_DTYPE)

    def proj(a, b):
        return _dot(xb, w_ref[:, a:b])

    ca, sla, sha = ca_ref[...], sla_ref[...], sha_ref[...]
    ci, sli, shi = ci_ref[...], sli_ref[...], shi_ref[...]

    hq = proj(_C_Q, _C_K)
    for j in range(hq.shape[1] // LANES):
        t = _rope_tile(hq[:, j * LANES:(j + 1) * LANES], ca, sla, sha, ROPE_DIM // 2)
        q_ref[:, j * LANES:(j + 1) * LANES] = t.astype(q_ref.dtype)
    hk = proj(_C_K, _C_V)
    for j in range(hk.shape[1] // LANES):
        t = _rope_tile(hk[:, j * LANES:(j + 1) * LANES], ca, sla, sha, ROPE_DIM // 2)
        k_ref[:, j * LANES:(j + 1) * LANES] = t
        kb_ref[:, j * LANES:(j + 1) * LANES] = t.astype(kb_ref.dtype)
    hv = proj(_C_V, _C_QI)
    v_ref[...] = hv
    vb_ref[...] = hv.astype(vb_ref.dtype)
    hqi = proj(_C_QI, _C_KW)
    for j in range(hqi.shape[1] // LANES):
        t = _rope_tile(hqi[:, j * LANES:(j + 1) * LANES], ci, sli, shi, IDX_ROPE_DIM // 2)
        qi_ref[:, j * LANES:(j + 1) * LANES] = t.astype(qi_ref.dtype)
    hkw = proj(_C_KW, _C_XR)
    t = _rope_tile(hkw, ci, sli, shi, IDX_ROPE_DIM // 2)
    ki_ref[...] = t[:, :IDX_DIM]
    kib_ref[...] = t[:, :IDX_DIM].astype(kib_ref.dtype)
    wi_ref[...] = hkw[:, IDX_DIM:IDX_DIM + N_IDX_HEADS]
    xr_ref[...] = proj(_C_XR, _C_GR)
    gr_ref[...] = proj(_C_GR, _C_GA)
    ga_ref[...] = proj(_C_GA, _C_GB)
    gb_ref[...] = proj(_C_GB, _C_END)


def _rope_tables(pos, rot_dim, head_dim):
    half = rot_dim // 2
    inv = ROPE_THETA ** (-jnp.arange(half, dtype=f32) / half)
    ang = pos.astype(f32)[:, None] * inv[None, :]
    cos, sin = jnp.cos(ang), jnp.sin(ang)
    n = pos.shape[0]
    one = jnp.ones((n, head_dim - rot_dim), f32)
    zero_h = jnp.zeros((n, half), f32)
    zero_r = jnp.zeros((n, head_dim - rot_dim), f32)
    c = jnp.concatenate([cos, cos, one], axis=1)
    s_lo = jnp.concatenate([-sin, zero_h, zero_r], axis=1)
    s_hi = jnp.concatenate([zero_h, sin, zero_r], axis=1)
    rep = LANES // head_dim
    return tuple(jnp.tile(t, (1, rep)) for t in (c, s_lo, s_hi))


def _pack_w_in(w_in):
    d = w_in.shape[0]
    pad = jnp.zeros((d, LANES - IDX_DIM - N_IDX_HEADS), w_in.dtype)
    split = 1024 + 256 + 256 + 512 + IDX_DIM + N_IDX_HEADS
    return jnp.concatenate([w_in[:, :split], pad, w_in[:, split:]], axis=1).astype(MXU_DTYPE)


def _in_projection(x2d, w_packed, tabs_a, tabs_i, tm):
    n, d = x2d.shape
    nt = tabs_a[0].shape[0] // tm
    row = lambda i: (i, 0)
    tab = lambda i: (i % nt, 0)
    out_cols = [(1024, MXU_DTYPE), (256, f32), (256, MXU_DTYPE), (256, f32), (256, MXU_DTYPE),
                (512, MXU_DTYPE), (IDX_DIM, f32), (IDX_DIM, MXU_DTYPE), (N_IDX_HEADS, f32),
                (1024, f32), (1024, f32), (1024, f32), (1024, f32)]
    return pl.pallas_call(
        _inproj_kernel,
        grid=(n // tm,),
        in_specs=[pl.BlockSpec((tm, d), row), _resident(w_packed.shape, lambda i: (0, 0))]
                 + [pl.BlockSpec((tm, LANES), tab)] * 6,
        out_specs=[pl.BlockSpec((tm, c), row) for c, _ in out_cols],
        out_shape=[jax.ShapeDtypeStruct((n, c), dt) for c, dt in out_cols],
        compiler_params=_cparams(("parallel",)),
        name="in_projection",
    )(x2d, w_packed, *tabs_a, *tabs_i)


MASKED_LOGIT = -2.0 ** 126
MAX_FLOOR = -2.0 ** 120
UNKNOWN_COUNT = 1e9
MIN_NORMAL = 2.0 ** -126


SEARCH_FALLBACK = 40


def _score_key(s):
    bits = pltpu.bitcast(s, i32)
    return bits ^ ((bits >> 31) & 0x7FFFFFFF)


def _key_score(k):
    return pltpu.bitcast(k ^ ((k >> 31) & 0x7FFFFFFF), f32)


def _dsa_kernel(q_ref, qi_ref, wi_ref, kit_ref, kt_ref, v_ref, o_ref,
                keys_scr, kaug_scr, sa_scr, sb_scr, ma_scr, mb_scr, vaug_scr, m_scr, acc_scr, upper_scr,
                *, qb, kb, s_true, pos0, n_sel):
    i = pl.program_id(1)

    @pl.when(i == 0)
    def _():
        upper_scr[...] = jnp.where(lax.broadcasted_iota(i32, (kb, kb), 0) < lax.broadcasted_iota(i32, (kb, kb), 1),
                                   1.0, 0.0).astype(MXU_DTYPE)
        ones_col = jnp.where(lax.broadcasted_iota(i32, (kb, HEAD_DIM), 1) == 0, 1.0, 0.0).astype(MXU_DTYPE)
        for g in range(N_KV_HEADS):
            vaug_scr[g, :, HEAD_DIM:] = ones_col

    row = lax.broadcasted_iota(i32, (qb, 1), 0)
    pos = pos0 + i * qb + row
    vis_end = jnp.minimum((pos // CHUNK + 1) * CHUNK, s_true)
    pos_last = pos0 + i * qb + (qb - 1)
    kend = jnp.minimum((pos_last // CHUNK + 1) * CHUNK, s_true)
    nkb = (kend + kb - 1) // kb
    lane = lax.broadcasted_iota(i32, (qb, kb), 1)

    qi = qi_ref[...]
    wi = wi_ref[...]
    qi_rows = jnp.concatenate([qi[:, h * IDX_DIM:(h + 1) * IDX_DIM] for h in range(N_IDX_HEADS)], axis=0)

    def score_block(b, carry):
        kit = kit_ref[b]
        sc = jnp.zeros((qb, kb), f32)
        d_all = _dot(qi_rows, kit)
        for h in range(N_IDX_HEADS):
            sc = sc + wi[:, h:h + 1] * jnp.maximum(d_all[h * qb:(h + 1) * qb, :], 0.0)
        sc = sc * IDX_SCALE
        sc = jnp.where(jnp.abs(sc) < MIN_NORMAL, 0.0, sc)
        adm = b * kb + lane < vis_end
        keys_scr[b] = jnp.where(adm, _score_key(sc), INT_MIN)
        smax, smin = carry
        hi_part = jnp.where(adm, sc, NEG_INF)
        lo_part = jnp.where(adm, sc, -NEG_INF)
        for t in range(kb // LANES):
            smax = jnp.maximum(smax, hi_part[:, t * LANES:(t + 1) * LANES])
            smin = jnp.minimum(smin, lo_part[:, t * LANES:(t + 1) * LANES])
        return smax, smin

    smax, smin = lax.fori_loop(0, nkb, score_block,
                               (jnp.full((qb, LANES), NEG_INF, f32), jnp.full((qb, LANES), -NEG_INF, f32)))
    smax = jnp.max(smax, axis=1, keepdims=True)
    smin = jnp.min(smin, axis=1, keepdims=True)

    def count(pred):
        def body(b, acc):
            for t in range(kb // LANES):
                hit = pred(keys_scr[b, :, t * LANES:(t + 1) * LANES], b * kb + t * LANES)
                acc = acc + jnp.where(hit, 1.0, 0.0)
            return acc
        acc = lax.fori_loop(0, nkb, body, jnp.zeros((qb, LANES), f32))
        return jnp.sum(acc, axis=1, keepdims=True)

    def wide(x):
        return jnp.broadcast_to(x, (qb, LANES))

    nsel = jnp.float32(n_sel)

    def unsettled(ct):
        return jnp.max(jnp.abs(ct - nsel)) > 0.0

    def search_cond(carry):
        t, lo, hi, clo = carry[:4]
        open_rows = jnp.where((clo == nsel) | (hi == lo + 1), 0.0, 1.0)
        return jnp.max(open_rows) > 0.0

    def search_one(t, lo, hi, clo, glo, ghi, side):
        vlo, vhi = _key_score(lo), _key_score(hi - 1)
        frac = glo / (glo - ghi)
        cand = _score_key(vlo + (vhi - vlo) * frac)
        cand = jnp.where(t >= SEARCH_FALLBACK, lo + jnp.right_shift(hi - lo, 1), cand)
        cand = jnp.where((t == 0) & (lo < 0) & (hi > 0), 0, cand)
        cand = jnp.where((t == 1) & (lo <= 0) & (hi > 1), 1, cand)
        cand = jnp.maximum(lo + 1, jnp.minimum(cand, hi - 1))
        cw = wide(cand)
        c = count(lambda k, c0: k >= cw)
        ok = c >= nsel
        g = jnp.log(jnp.maximum(c, 0.5) / nsel)
        ghi = jnp.where(ok & (side > 0), 0.5 * ghi, ghi)
        glo = jnp.where(jnp.logical_not(ok) & (side < 0), 0.5 * glo, glo)
        return (jnp.where(ok, cand, lo), jnp.where(ok, hi, cand), jnp.where(ok, c, clo),
                jnp.where(ok, g, glo), jnp.where(ok, ghi, g), jnp.where(ok, 1, -1))

    def search_step(carry):
        t, state = carry[0], carry[1:]
        state = search_one(t, *state)
        state = search_one(t + 1, *state)
        return (t + 2,) + state

    nvis = vis_end.astype(f32)
    short = nvis <= nsel
    lo0 = jnp.where(short, INT_MIN, _score_key(smin))
    hi0 = jnp.where(short, INT_MIN + 1, _score_key(smax) + 1)
    clo0 = jnp.where(short, UNKNOWN_COUNT, nvis)
    glo0 = jnp.log(jnp.maximum(nvis, nsel) / nsel)
    ghi0 = jnp.full((qb, 1), np.log(0.5 / n_sel), f32)
    _, tau, _, ctau, _, _, _ = lax.while_loop(
        search_cond, search_step, (jnp.int32(0), lo0, hi0, clo0, glo0, ghi0, jnp.zeros((qb, 1), i32)))

    @pl.when(unsettled(ctau))
    def _():
        tw = wide(tau)
        need = nsel - count(lambda k, c0: k > tw)
        need = jnp.where(tau == INT_MIN, UNKNOWN_COUNT, need)

        def demote(b, seen):
            k = keys_scr[b]
            eq = k == tau
            before = _dot(jnp.where(eq, 1.0, 0.0).astype(MXU_DTYPE), upper_scr[...]) + seen
            keys_scr[b] = jnp.where(eq & (before >= need), tau - 1, k)
            return seen + jnp.sum(jnp.where(eq, 1.0, 0.0), axis=1, keepdims=True)

        lax.fori_loop(0, nkb, demote, jnp.zeros((qb, 1), f32))

    tau_sel = jnp.maximum(tau, INT_MIN + 1)

    q = q_ref[...]
    rq = lax.broadcasted_iota(i32, (GROUP * qb, qb), 0) % qb
    onehot = jnp.where(rq == lax.broadcasted_iota(i32, (GROUP * qb, qb), 1), 1.0, 0.0).astype(MXU_DTYPE)
    qaug = [jnp.concatenate([jnp.concatenate([q[:, (g * GROUP + j) * HEAD_DIM:(g * GROUP + j + 1) * HEAD_DIM]
                                              for j in range(GROUP)], axis=0), onehot], axis=1)
            for g in range(N_KV_HEADS)]
    m_scr[...] = jnp.full(m_scr.shape, MAX_FLOOR, f32)
    acc_scr[...] = jnp.zeros(acc_scr.shape, f32)
    c2 = HEAD_DIM ** -0.5 * LOG2_E

    def logits(b, s_ref, smax_ref):
        k = keys_scr[b]
        mask_rows = jnp.where(k >= tau_sel, 0.0, MASKED_LOGIT).astype(MXU_DTYPE)
        for g in range(N_KV_HEADS):
            kaug_scr[g, :HEAD_DIM, :] = kt_ref[b, g * HEAD_DIM:(g + 1) * HEAD_DIM, :]
            kaug_scr[g, HEAD_DIM:, :] = mask_rows
        for g in range(N_KV_HEADS):
            s = _dot(qaug[g], kaug_scr[g]) * c2
            s_ref[g] = s
            smax_ref[g] = jnp.max(s, axis=1, keepdims=True)

    def update(b, s_ref, smax_ref):
        for g in range(N_KV_HEADS):
            m_old = m_scr[g]
            m_new = jnp.maximum(m_old, smax_ref[g])
            p = jnp.exp2(s_ref[g] - m_new)
            alpha = jnp.exp2(m_old - m_new)
            vaug_scr[g, :, :HEAD_DIM] = v_ref[b, :, g * HEAD_DIM:(g + 1) * HEAD_DIM]
            pv = _dot(p.astype(MXU_DTYPE), vaug_scr[g])
            acc_scr[g] = alpha * acc_scr[g] + pv
            m_scr[g] = m_new

    logits(0, sa_scr, ma_scr)

    def block_pair(ip, carry):
        b0 = 2 * ip
        logits(jnp.minimum(b0 + 1, nkb - 1), sb_scr, mb_scr)
        update(b0, sa_scr, ma_scr)

        @pl.when(b0 + 1 < nkb)
        def _():
            logits(jnp.minimum(b0 + 2, nkb - 1), sa_scr, ma_scr)
            update(b0 + 1, sb_scr, mb_scr)
        return carry

    lax.fori_loop(0, (nkb + 1) // 2, block_pair, 0)

    for g in range(N_KV_HEADS):
        o = acc_scr[g, :, :HEAD_DIM] / acc_scr[g, :, HEAD_DIM:HEAD_DIM + 1]
        for j in range(GROUP):
            h = g * GROUP + j
            o_ref[:, h * HEAD_DIM:(h + 1) * HEAD_DIM] = o[j * qb:(j + 1) * qb, :]


def _dsa_attention(q, qi, wi, ki_all, k_all, v_all, *, qb, kb, s_true, pos0):
    bsz, t, _ = q.shape
    s_pad = k_all.shape[1]
    nk = s_pad // kb
    n_sel = min(TOPK_KEYS, s_true // 4)
    kit = ki_all.reshape(bsz, nk, kb, IDX_DIM).transpose(0, 1, 3, 2)
    kt = k_all.reshape(bsz, nk, kb, N_KV_HEADS * HEAD_DIM).transpose(0, 1, 3, 2)
    vv = v_all.reshape(bsz, nk, kb, N_KV_HEADS * HEAD_DIM)
    qrow = lambda b, i: (b, i, 0)
    kmap = lambda b, i: (b, 0, 0, 0)
    kern = functools.partial(_dsa_kernel, qb=qb, kb=kb, s_true=s_true, pos0=pos0, n_sel=n_sel)
    return pl.pallas_call(
        kern,
        grid=(bsz, t // qb),
        in_specs=[pl.BlockSpec((None, qb, q.shape[2]), qrow),
                  pl.BlockSpec((None, qb, qi.shape[2]), qrow),
                  pl.BlockSpec((None, qb, wi.shape[2]), qrow),
                  _resident((None, nk, IDX_DIM, kb), kmap),
                  _resident((None, nk, N_KV_HEADS * HEAD_DIM, kb), kmap),
                  _resident((None, nk, kb, N_KV_HEADS * HEAD_DIM), kmap)],
        out_specs=pl.BlockSpec((None, qb, q.shape[2]), qrow),
        out_shape=jax.ShapeDtypeStruct(q.shape, f32),
        scratch_shapes=[pltpu.VMEM((nk, qb, kb), i32),
                        pltpu.VMEM((N_KV_HEADS, HEAD_DIM + qb, kb), MXU_DTYPE),
                        pltpu.VMEM((N_KV_HEADS, GROUP * qb, kb), f32),
                        pltpu.VMEM((N_KV_HEADS, GROUP * qb, kb), f32),
                        pltpu.VMEM((N_KV_HEADS, GROUP * qb, 1), f32),
                        pltpu.VMEM((N_KV_HEADS, GROUP * qb, 1), f32),
                        pltpu.VMEM((N_KV_HEADS, kb, 2 * HEAD_DIM), MXU_DTYPE),
                        pltpu.VMEM((N_KV_HEADS, GROUP * qb, 1), f32),
                        pltpu.VMEM((N_KV_HEADS, GROUP * qb, 2 * HEAD_DIM), f32),
                        pltpu.VMEM((kb, kb), MXU_DTYPE)],
        compiler_params=_cparams(("parallel", "arbitrary")),
        name="dsa_attention",
    )(q, qi, wi, kit, kt, vv)


def _expm1(y):
    u = jnp.exp(y)
    um1 = u - 1.0
    return jnp.where(um1 == 0.0, y, jnp.where(um1 == -1.0, -1.0, um1 * y / jnp.log(u)))


def _rglru_kernel(xr_ref, gr_ref, cs_ref, h0_ref, wc_ref, bc_ref, wax_ref, ba_ref, bx_ref, lam_ref,
                  rnn_ref, hlast_ref, xp_scr, a_scr, b_scr, h_scr, *, reset_first):
    t = pl.program_id(1)
    tt, c = xr_ref.shape
    lead = SUBLANES

    @pl.when(t == 0)
    def _():
        xp_scr[lead - (CONV_W - 1):lead, :] = cs_ref[...]
        h_scr[...] = h0_ref[...]

    xp_scr[lead:lead + tt, :] = xr_ref[...]
    conv = bc_ref[...]
    for j in range(CONV_W):
        off = lead - (CONV_W - 1) + j
        conv = conv + xp_scr[off:off + tt, :] * wc_ref[j:j + 1, :]
    xp_scr[lead - (CONV_W - 1):lead, :] = xp_scr[lead + tt - (CONV_W - 1):lead + tt, :]

    cb = conv.astype(MXU_DTYPE)
    sp = lam_ref[...]
    grow = t * tt + lax.broadcasted_iota(i32, (tt, 1), 0)
    for g in range(c // RNN_GROUP):
        sl = slice(g * RNN_GROUP, (g + 1) * RNN_GROUP)
        z = _dot(cb[:, sl], wax_ref[g])
        r = jax.nn.sigmoid(z[:, :RNN_GROUP] + ba_ref[:, sl])
        ig = jax.nn.sigmoid(z[:, RNN_GROUP:] + bx_ref[:, sl])
        log_a = -LRU_C * r * sp[:, sl]
        mult = jnp.sqrt(-_expm1(2.0 * log_a))
        if reset_first:
            mult = jnp.where(grow == 0, 1.0, mult)
        a_scr[:, sl] = jnp.exp(log_a)
        b_scr[:, sl] = mult * (ig * conv[:, sl])

    def rows(jb, h):
        base = pl.multiple_of(jb * SUBLANES, SUBLANES)
        for u in range(SUBLANES):
            h = a_scr[pl.ds(base + u, 1), :] * h + b_scr[pl.ds(base + u, 1), :]
            rnn_ref[pl.ds(base + u, 1), :] = h
        return h

    h = lax.fori_loop(0, tt // SUBLANES, rows, h_scr[...])
    h_scr[...] = h
    rnn_ref[...] = rnn_ref[...] * jax.nn.gelu(gr_ref[...])

    @pl.when(t == pl.num_programs(1) - 1)
    def _():
        hlast_ref[...] = h


def _blockdiag_groups(w):
    per = RNN_GROUP // RNN_BLOCK
    g = w.reshape(N_RNN_BLOCKS // per, per, RNN_BLOCK, RNN_BLOCK)
    eye = jnp.eye(per, dtype=w.dtype)
    return jnp.einsum('gacd,ab->gacbd', g, eye).reshape(N_RNN_BLOCKS // per, RNN_GROUP, RNN_GROUP)


def _rglru(xr, gr, conv_state, h0, w_conv, b_conv, w_a, b_a, w_x, b_x, lam, *, tt, reset_first):
    bsz, t, c = xr.shape
    wax = jnp.concatenate([_blockdiag_groups(w_a), _blockdiag_groups(w_x)], axis=2).astype(MXU_DTYPE)
    sp = jax.nn.softplus(-lam.astype(f32)).reshape(1, c)
    seq = lambda b, i: (b, i, 0)
    per_b = lambda b, i: (b, 0, 0)
    const2 = lambda b, i: (0, 0)
    const3 = lambda b, i: (0, 0, 0)
    kern = functools.partial(_rglru_kernel, reset_first=reset_first)
    return pl.pallas_call(
        kern,
        grid=(bsz, t // tt),
        in_specs=[pl.BlockSpec((None, tt, c), seq), pl.BlockSpec((None, tt, c), seq),
                  pl.BlockSpec((None, CONV_W - 1, c), per_b), pl.BlockSpec((None, 1, c), per_b),
                  pl.BlockSpec((CONV_W, c), const2), pl.BlockSpec((1, c), const2),
                  pl.BlockSpec(wax.shape, const3), pl.BlockSpec((1, c), const2),
                  pl.BlockSpec((1, c), const2), pl.BlockSpec((1, c), const2)],
        out_specs=[pl.BlockSpec((None, tt, c), seq), pl.BlockSpec((None, 1, c), per_b)],
        out_shape=[jax.ShapeDtypeStruct((bsz, t, c), f32), jax.ShapeDtypeStruct((bsz, 1, c), f32)],
        scratch_shapes=[pltpu.VMEM((tt + SUBLANES, c), f32), pltpu.VMEM((tt, c), f32),
                        pltpu.VMEM((tt, c), f32), pltpu.VMEM((1, c), f32)],
        compiler_params=_cparams(("parallel", "arbitrary")),
        name="rglru",
    )(xr, gr, conv_state, h0.reshape(bsz, 1, c), w_conv, b_conv.reshape(1, c), wax,
      b_a.reshape(1, c), b_x.reshape(1, c), sp)


def _layer_norm(z, g, b):
    mu = jnp.mean(z, axis=-1, keepdims=True)
    var = jnp.mean(jnp.square(z - mu), axis=-1, keepdims=True)
    return (z - mu) * lax.rsqrt(var + LN_EPS) * g + b


def _merge_kernel(attn_ref, rnn_ref, ga_ref, gb_ref, x_ref, wo_ref, g1_ref, b1_ref, wr_ref, br_ref,
                  x1_ref, x1b_ref, cw_ref, *, dn_alpha):
    merged = jax.nn.sigmoid(ga_ref[...]) * attn_ref[...] + jax.nn.sigmoid(gb_ref[...]) * rnn_ref[...]
    m = _dot(merged.astype(MXU_DTYPE), wo_ref[...])
    x1 = _layer_norm(dn_alpha * x_ref[...] + m, g1_ref[...], b1_ref[...])
    x1_ref[...] = x1
    x1b = x1.astype(MXU_DTYPE)
    x1b_ref[...] = x1b
    logits = _dot(x1b, wr_ref[...]) + br_ref[...]
    ne = logits.shape[1]
    eidx = lax.broadcasted_iota(i32, logits.shape, 1).astype(f32)
    work = logits
    chosen = jnp.zeros(logits.shape, jnp.bool_)
    top = None
    for kk in range(TOP_K):
        mx = jnp.max(work, axis=1, keepdims=True)
        if kk == 0:
            top = mx
        first = jnp.min(jnp.where(work == mx, eidx, ne), axis=1, keepdims=True)
        pick = eidx == first
        chosen = chosen | pick
        work = jnp.where(pick, NEG_INF, work)
    e = jnp.where(chosen, jnp.exp(logits - top), 0.0)
    cw_ref[...] = e / jnp.sum(e, axis=1, keepdims=True)


def _merge_ln_router(attn, rnn, ga, gb, x2d, w_out, ln_g, ln_b, w_router, b_router, *, tm, dn_alpha):
    n, d = x2d.shape
    ne = w_router.shape[1]
    row = lambda i: (i, 0)
    const = lambda i: (0, 0)
    kern = functools.partial(_merge_kernel, dn_alpha=dn_alpha)
    return pl.pallas_call(
        kern,
        grid=(n // tm,),
        in_specs=[pl.BlockSpec((tm, d), row)] * 5
                 + [pl.BlockSpec((d, d), const), pl.BlockSpec((1, d), const), pl.BlockSpec((1, d), const),
                    pl.BlockSpec((d, ne), const), pl.BlockSpec((1, ne), const)],
        out_specs=[pl.BlockSpec((tm, d), row), pl.BlockSpec((tm, d), row), pl.BlockSpec((tm, ne), row)],
        out_shape=[jax.ShapeDtypeStruct((n, d), f32), jax.ShapeDtypeStruct((n, d), MXU_DTYPE),
                   jax.ShapeDtypeStruct((n, ne), f32)],
        compiler_params=_cparams(("parallel",)),
        name="merge_ln_router",
    )(attn, rnn, ga, gb, x2d, w_out.astype(MXU_DTYPE), ln_g.reshape(1, d), ln_b.reshape(1, d),
      w_router.astype(MXU_DTYPE), b_router.reshape(1, ne))


MOE_CHUNK = 128


def _moe_kernel(xb_ref, cwt_ref, wg_ref, bg_ref, wu_ref, bu_ref, wd_ref, bd_ref, o_ref, rankt_scr):
    e = pl.program_id(1)
    ne, tm = cwt_ref.shape

    @pl.when(e == 0)
    def _():
        selt = jnp.where(cwt_ref[...] > 0.0, 1.0, 0.0)
        upper = jnp.where(lax.broadcasted_iota(i32, (tm, tm), 0) < lax.broadcasted_iota(i32, (tm, tm), 1),
                          1.0, 0.0).astype(MXU_DTYPE)
        rankt = _dot(selt.astype(MXU_DTYPE), upper)
        rankt_scr[...] = jnp.where(selt > 0.0, rankt, -1.0)
        o_ref[...] = jnp.zeros(o_ref.shape, f32)

    rrow = rankt_scr[pl.ds(e, 1), :]
    grow = cwt_ref[pl.ds(e, 1), :]
    cnt = jnp.max(rrow).astype(i32) + 1

    def run_chunk(first_rank, size):
        lo = first_rank.astype(f32)
        hit = rrow - lo == lax.broadcasted_iota(i32, (size, tm), 0).astype(f32)
        gather = jnp.where(hit, 1.0, 0.0).astype(MXU_DTYPE)
        gcomp = jnp.sum(jnp.where(hit, grow, 0.0), axis=1, keepdims=True)
        xg = _dot(gather, xb_ref[...]).astype(MXU_DTYPE)
        g = _dot(xg, wg_ref[...]) + bg_ref[...]
        u = _dot(xg, wu_ref[...]) + bu_ref[...]
        g = jnp.minimum(g, SWIGLU_LIMIT)
        u = jnp.clip(u, -SWIGLU_LIMIT, SWIGLU_LIMIT)
        hmid = g * jax.nn.sigmoid(SWIGLU_ALPHA * g) * (u + 1.0)
        y = (_dot(hmid.astype(MXU_DTYPE), wd_ref[...]) + bd_ref[...]) * gcomp
        y_hi = y.astype(MXU_DTYPE)
        y_lo = (y - y_hi.astype(f32)).astype(MXU_DTYPE)
        o_ref[...] += lax.dot_general(jnp.concatenate([gather, gather], axis=0),
                                      jnp.concatenate([y_hi, y_lo], axis=0),
                                      (((0,), (0,)), ((), ())), preferred_element_type=f32)

    def chunk(cidx, carry):
        run_chunk(cidx * MOE_CHUNK, MOE_CHUNK)
        return carry

    lax.fori_loop(0, (cnt + MOE_CHUNK - 1) // MOE_CHUNK, chunk, 0)


def _moe(x1b, cw, w_gate, b_gate, w_up, b_up, w_down, b_down, *, tm):
    n, d = x1b.shape
    ne = cw.shape[1]
    dff = w_gate.shape[2]
    row = lambda i, e: (i, 0)
    wmap = lambda i, e: (e, 0, 0)
    return pl.pallas_call(
        _moe_kernel,
        grid=(n // tm, ne),
        in_specs=[pl.BlockSpec((tm, d), row), pl.BlockSpec((ne, tm), lambda i, e: (0, i)),
                  pl.BlockSpec((None, d, dff), wmap), pl.BlockSpec((None, 1, dff), wmap),
                  pl.BlockSpec((None, d, dff), wmap), pl.BlockSpec((None, 1, dff), wmap),
                  pl.BlockSpec((None, dff, d), wmap), pl.BlockSpec((None, 1, d), wmap)],
        out_specs=pl.BlockSpec((tm, d), row),
        out_shape=jax.ShapeDtypeStruct((n, d), f32),
        scratch_shapes=[pltpu.VMEM((ne, tm), f32)],
        compiler_params=_cparams(("parallel", "arbitrary")),
        name="moe_experts",
    )(x1b, cw.T, w_gate.astype(MXU_DTYPE), b_gate.reshape(ne, 1, dff), w_up.astype(MXU_DTYPE),
      b_up.reshape(ne, 1, dff), w_down.astype(MXU_DTYPE), b_down.reshape(ne, 1, d))


def _final_kernel(x1_ref, ffn_ref, p_ref, g2_ref, b2_ref, wpg_ref, wpp_ref, y_ref, *, dn_alpha):
    x2 = _layer_norm(dn_alpha * x1_ref[...] + ffn_ref[...], g2_ref[...], b2_ref[...])
    gate = jax.nn.sigmoid(_dot(x2.astype(MXU_DTYPE), wpg_ref[...]))
    emb = _dot(p_ref[...].astype(MXU_DTYPE), wpp_ref[...])
    y_ref[...] = x2 + gate * emb


def _final(x1, ffn, p2d, ln_g, ln_b, w_ple_gate, w_ple_proj, *, tm, dn_alpha):
    n, d = x1.shape
    dp = p2d.shape[1]
    row = lambda i: (i, 0)
    const = lambda i: (0, 0)
    kern = functools.partial(_final_kernel, dn_alpha=dn_alpha)
    return pl.pallas_call(
        kern,
        grid=(n // tm,),
        in_specs=[pl.BlockSpec((tm, d), row), pl.BlockSpec((tm, d), row), pl.BlockSpec((tm, dp), row),
                  pl.BlockSpec((1, d), const), pl.BlockSpec((1, d), const),
                  pl.BlockSpec((d, d), const), pl.BlockSpec((dp, d), const)],
        out_specs=pl.BlockSpec((tm, d), row),
        out_shape=jax.ShapeDtypeStruct((n, d), f32),
        compiler_params=_cparams(("parallel",)),
        name="ln2_ple",
    )(x1, ffn, p2d, ln_g.reshape(1, d), ln_b.reshape(1, d), w_ple_gate.astype(MXU_DTYPE),
      w_ple_proj.astype(MXU_DTYPE))


def _pick_tile(n, pref):
    t = min(pref, n)
    while n % t:
        t //= 2
    return t


def _layer(x, p, pos0, past_k, past_v, past_ki, conv_state, h0, prm, depth):
    (w_in, w_conv, b_conv, w_a, b_a, w_x, b_x, lam, w_out, ln1_g, ln1_b, w_router, b_router,
     w_gate, b_gate, w_up, b_up, w_down, b_down, ln2_g, ln2_b, w_ple_gate, w_ple_proj) = prm
    bsz, t, d = x.shape
    n = bsz * t
    past = past_k.shape[1]
    dn_alpha = (2 * depth) ** 0.25

    tm = _pick_tile(n, 512)
    pos = pos0 + jnp.arange(t)
    tabs_a = _rope_tables(pos, ROPE_DIM, HEAD_DIM)
    tabs_i = _rope_tables(pos, IDX_ROPE_DIM, IDX_DIM)
    if t % tm:
        rep = tm // t
        tabs_a = tuple(jnp.tile(a, (rep, 1)) for a in tabs_a)
        tabs_i = tuple(jnp.tile(a, (rep, 1)) for a in tabs_i)
    (q, k, kb, v, vb, qi, ki, kib, wi, xr, gr, ga, gb) = _in_projection(
        x.reshape(n, d), _pack_w_in(w_in), tabs_a, tabs_i, tm)

    s_true = past + t
    qb = min(Q_BLOCK, t)
    kblk = 1024 if s_true % 1024 == 0 else 3 * LANES
    s_pad = -(-s_true // kblk) * kblk

    def keys_by_position(past_arr, new_arr):
        new_arr = new_arr.reshape(bsz, t, -1)
        parts = [new_arr]
        if past:
            parts.insert(0, past_arr.reshape(bsz, past, new_arr.shape[2]).astype(MXU_DTYPE))
        if s_pad > s_true:
            parts.append(jnp.zeros((bsz, s_pad - s_true, new_arr.shape[2]), MXU_DTYPE))
        return jnp.concatenate(parts, axis=1)

    attn = _dsa_attention(q.reshape(bsz, t, -1), qi.reshape(bsz, t, -1), wi.reshape(bsz, t, -1),
                          keys_by_position(past_ki, kib), keys_by_position(past_k, kb),
                          keys_by_position(past_v, vb), qb=qb, kb=kblk, s_true=s_true, pos0=pos0)

    tt = _pick_tile(t, 512)
    xr3 = xr.reshape(bsz, t, -1)
    rnn, h_last = _rglru(xr3, gr.reshape(bsz, t, -1), conv_state, h0, w_conv, b_conv, w_a, b_a, w_x, b_x,
                         lam, tt=tt, reset_first=(pos0 == 0))
    new_conv = jnp.concatenate([conv_state, xr3], axis=1)[:, -(CONV_W - 1):]

    x1, x1b, cw = _merge_ln_router(attn.reshape(n, d), rnn.reshape(n, d), ga, gb, x.reshape(n, d), w_out,
                                   ln1_g, ln1_b, w_router, b_router, tm=tm, dn_alpha=dn_alpha)
    ffn = _moe(x1b, cw, w_gate, b_gate, w_up, b_up, w_down, b_down, tm=_pick_tile(n, 1024))
    y = _final(x1, ffn, p.reshape(n, -1), ln2_g, ln2_b, w_ple_gate, w_ple_proj, tm=tm, dn_alpha=dn_alpha)

    return (y.reshape(bsz, t, d), k.reshape(bsz, t, N_KV_HEADS, HEAD_DIM), v.reshape(bsz, t, N_KV_HEADS, HEAD_DIM),
            ki.reshape(bsz, t, IDX_DIM), new_conv, h_last.reshape(bsz, -1))


def kernel(x_prompt, x_sample, cache_k, cache_v, cache_kidx, state_conv, state_h, p_prompt, p_sample,
           w_in, w_conv, b_conv, w_a, b_a, w_x, b_x, lru_lambda, w_out, ln1_g, ln1_b, w_router, b_router,
           w_gate, b_gate, w_up, b_up, w_down, b_down, ln2_g, ln2_b, w_ple_gate, w_ple_proj):
    depth = w_in.shape[0]
    bp = x_prompt.shape[0]
    past_len = cache_k.shape[2]
    dt = x_prompt.dtype
    empty_kv = jnp.zeros((bp, 0, N_KV_HEADS, HEAD_DIM), dt)
    empty_ki = jnp.zeros((bp, 0, IDX_DIM), dt)
    zero_conv = jnp.zeros((bp, CONV_W - 1, x_prompt.shape[2]), dt)
    zero_h = jnp.zeros((bp, x_prompt.shape[2]), dt)

    yp, ys = x_prompt, x_sample
    outs_p, outs_s = [], []
    for l in range(depth):
        prm = (w_in[l], w_conv[l], b_conv[l], w_a[l], b_a[l], w_x[l], b_x[l], lru_lambda[l], w_out[l],
               ln1_g[l], ln1_b[l], w_router[l], b_router[l], w_gate[l], b_gate[l], w_up[l], b_up[l],
               w_down[l], b_down[l], ln2_g[l], ln2_b[l], w_ple_gate[l], w_ple_proj[l])
        yp, *rest_p = _layer(yp, p_prompt[l], 0, empty_kv, empty_kv, empty_ki, zero_conv, zero_h, prm, depth)
        ys, *rest_s = _layer(ys, p_sample[l], past_len, cache_k[l], cache_v[l], cache_kidx[l],
                             state_conv[l], state_h[l], prm, depth)
        outs_p.append(rest_p)
        outs_s.append(rest_s)

    stack = lambda outs, j: jnp.stack([o[j] for o in outs])
    return (yp, ys) + tuple(stack(outs_p, j) for j in range(5)) + tuple(stack(outs_s, j) for j in range(5))
```

```python
import functools

import jax
import jax.numpy as jnp
import numpy as np
from jax import lax
from jax.experimental import pallas as pl
from jax.experimental.pallas import tpu as pltpu

f32 = jnp.float32
i32 = jnp.int32
MXU_DTYPE = jnp.bfloat16

CHUNK = 64
HEAD_DIM = 128
N_KV_HEADS = 2
GROUP = 4
ROPE_DIM = 32
ROPE_THETA = 500000.0
N_IDX_HEADS = 8
IDX_DIM = 64
IDX_ROPE_DIM = 16
IDX_SCALE = (IDX_DIM * N_IDX_HEADS) ** -0.5
TOPK_KEYS = 256
Q_BLOCK = 128
N_RNN_BLOCKS = 16
RNN_BLOCK = 64
RNN_GROUP = 256
CONV_W = 4
LRU_C = 8.0
N_EXPERTS = 32
TOP_K = 4
SWIGLU_LIMIT = 7.0
SWIGLU_ALPHA = 1.702
LN_EPS = 1e-5

LANES = 128
SUBLANES = 8
VMEM_LIMIT = 56 * 1024 * 1024
INT_MIN = -2 ** 31
NEG_INF = float("-inf")
LOG2_E = 1.4426950408889634


def _cparams(sem):
    return pltpu.CompilerParams(dimension_semantics=sem, vmem_limit_bytes=VMEM_LIMIT)


def _resident(shape, index_map):
    return pl.BlockSpec(shape, index_map, pipeline_mode=pl.Buffered(1))


def _dot(a, b):
    return jnp.dot(a, b, preferred_element_type=f32)


D_MODEL = N_KV_HEADS * GROUP * HEAD_DIM
D_KV = N_KV_HEADS * HEAD_DIM
D_QI = N_IDX_HEADS * IDX_DIM
_SEG_WIDTHS = (D_MODEL, D_KV, D_KV, D_QI, LANES, D_MODEL, D_MODEL, D_MODEL, D_MODEL)
_C_Q, _C_K, _C_V, _C_QI, _C_KW, _C_XR, _C_GR, _C_GA, _C_GB, _C_END = (
    int(c) for c in np.concatenate([[0], np.cumsum(_SEG_WIDTHS)]))


def _rope_tile(h, c, s_lo, s_hi, half):
    return h * c + pltpu.roll(h, half, 1) * s_hi + pltpu.roll(h, LANES - half, 1) * s_lo


def _inproj_kernel(x_ref, w_ref, ca_ref, sla_ref, sha_ref, ci_ref, sli_ref, shi_ref,
                   q_ref, k_ref, kb_ref, v_ref, vb_ref, qi_ref, ki_ref, kib_ref, wi_ref,
                   xr_ref, gr_ref, ga_ref, gb_ref):
    xb = x_ref[...].astype(MXU_DTYPE)

    def proj(a, b):
        return _dot(xb, w_ref[:, a:b])

    ca, sla, sha = ca_ref[...], sla_ref[...], sha_ref[...]
    ci, sli, shi = ci_ref[...], sli_ref[...], shi_ref[...]

    hq = proj(_C_Q, _C_K)
    for j in range(hq.shape[1] // LANES):
        t = _rope_tile(hq[:, j * LANES:(j + 1) * LANES], ca, sla, sha, ROPE_DIM // 2)
        q_ref[:, j * LANES:(j + 1) * LANES] = t.astype(q_ref.dtype)
    hk = proj(_C_K, _C_V)
    for j in range(hk.shape[1] // LANES):
        t = _rope_tile(hk[:, j * LANES:(j + 1) * LANES], ca, sla, sha, ROPE_DIM // 2)
        k_ref[:, j * LANES:(j + 1) * LANES] = t
        kb_ref[:, j * LANES:(j + 1) * LANES] = t.astype(kb_ref.dtype)
    hv = proj(_C_V, _C_QI)
    v_ref[...] = hv
    vb_ref[...] = hv.astype(vb_ref.dtype)
    hqi = proj(_C_QI, _C_KW)
    for j in range(hqi.shape[1] // LANES):
        t = _rope_tile(hqi[:, j * LANES:(j + 1) * LANES], ci, sli, shi, IDX_ROPE_DIM // 2)
        qi_ref[:, j * LANES:(j + 1) * LANES] = t.astype(qi_ref.dtype)
    hkw = proj(_C_KW, _C_XR)
    t = _rope_tile(hkw, ci, sli, shi, IDX_ROPE_DIM // 2)
    ki_ref[...] = t[:, :IDX_DIM]
    kib_ref[...] = t[:, :IDX_DIM].astype(kib_ref.dtype)
    wi_ref[...] = hkw[:, IDX_DIM:IDX_DIM + N_IDX_HEADS]
    xr_ref[...] = proj(_C_XR, _C_GR)
    gr_ref[...] = proj(_C_GR, _C_GA)
    ga_ref[...] = proj(_C_GA, _C_GB)
    gb_ref[...] = proj(_C_GB, _C_END)


def _rope_tables(pos, rot_dim, head_dim):
    half = rot_dim // 2
    inv = ROPE_THETA ** (-jnp.arange(half, dtype=f32) / half)
    ang = pos.astype(f32)[:, None] * inv[None, :]
    cos, sin = jnp.cos(ang), jnp.sin(ang)
    n = pos.shape[0]
    one = jnp.ones((n, head_dim - rot_dim), f32)
    zero_h = jnp.zeros((n, half), f32)
    zero_r = jnp.zeros((n, head_dim - rot_dim), f32)
    c = jnp.concatenate([cos, cos, one], axis=1)
    s_lo = jnp.concatenate([-sin, zero_h, zero_r], axis=1)
    s_hi = jnp.concatenate([zero_h, sin, zero_r], axis=1)
    rep = LANES // head_dim
    return tuple(jnp.tile(t, (1, rep)) for t in (c, s_lo, s_hi))


def _pack_w_in(w_in):
    d = w_in.shape[0]
    pad = jnp.zeros((d, LANES - IDX_DIM - N_IDX_HEADS), w_in.dtype)
    split = _C_KW + IDX_DIM + N_IDX_HEADS
    return jnp.concatenate([w_in[:, :split], pad, w_in[:, split:]], axis=1).astype(MXU_DTYPE)


def _in_projection(x2d, w_packed, tabs_a, tabs_i, tm):
    n, d = x2d.shape
    nt = tabs_a[0].shape[0] // tm
    row = lambda i: (i, 0)
    tab = lambda i: (i % nt, 0)
    out_cols = [(D_MODEL, MXU_DTYPE), (D_KV, f32), (D_KV, MXU_DTYPE), (D_KV, f32), (D_KV, MXU_DTYPE),
                (D_QI, MXU_DTYPE), (IDX_DIM, f32), (IDX_DIM, MXU_DTYPE), (N_IDX_HEADS, f32),
                (D_MODEL, f32), (D_MODEL, f32), (D_MODEL, f32), (D_MODEL, f32)]
    return pl.pallas_call(
        _inproj_kernel,
        grid=(n // tm,),
        in_specs=[pl.BlockSpec((tm, d), row), _resident(w_packed.shape, lambda i: (0, 0))]
                 + [pl.BlockSpec((tm, LANES), tab)] * 6,
        out_specs=[pl.BlockSpec((tm, c), row) for c, _ in out_cols],
        out_shape=[jax.ShapeDtypeStruct((n, c), dt) for c, dt in out_cols],
        compiler_params=_cparams(("parallel",)),
        name="in_projection",
    )(x2d, w_packed, *tabs_a, *tabs_i)


MASKED_LOGIT = -2.0 ** 126
MAX_FLOOR = -2.0 ** 120
UNKNOWN_COUNT = 1e9
MIN_NORMAL = 2.0 ** -126


SEARCH_FALLBACK = 40


def _score_key(s):
    bits = pltpu.bitcast(s, i32)
    return bits ^ ((bits >> 31) & 0x7FFFFFFF)


def _key_score(k):
    return pltpu.bitcast(k ^ ((k >> 31) & 0x7FFFFFFF), f32)


def _dsa_kernel(q_ref, qi_ref, wi_ref, kit_ref, kt_ref, v_ref, o_ref,
                keys_scr, kaug_scr, sa_scr, sb_scr, ma_scr, mb_scr, vaug_scr, m_scr, acc_scr, upper_scr,
                *, qb, kb, s_true, pos0, n_sel):
    i = pl.program_id(1)

    @pl.when(i == 0)
    def _():
        upper_scr[...] = jnp.where(lax.broadcasted_iota(i32, (kb, kb), 0) < lax.broadcasted_iota(i32, (kb, kb), 1),
                                   1.0, 0.0).astype(MXU_DTYPE)
        ones_col = jnp.where(lax.broadcasted_iota(i32, (kb, HEAD_DIM), 1) == 0, 1.0, 0.0).astype(MXU_DTYPE)
        for g in range(N_KV_HEADS):
            vaug_scr[g, :, HEAD_DIM:] = ones_col

    row = lax.broadcasted_iota(i32, (qb, 1), 0)
    pos = pos0 + i * qb + row
    vis_end = jnp.minimum((pos // CHUNK + 1) * CHUNK, s_true)
    pos_last = pos0 + i * qb + (qb - 1)
    kend = jnp.minimum((pos_last // CHUNK + 1) * CHUNK, s_true)
    nkb = (kend + kb - 1) // kb
    lane = lax.broadcasted_iota(i32, (qb, kb), 1)

    qi = qi_ref[...]
    wi = wi_ref[...]
    qi_rows = jnp.concatenate([qi[:, h * IDX_DIM:(h + 1) * IDX_DIM] for h in range(N_IDX_HEADS)], axis=0)

    def score_block(b, carry):
        kit = kit_ref[b]
        sc = jnp.zeros((qb, kb), f32)
        d_all = _dot(qi_rows, kit)
        for h in range(N_IDX_HEADS):
            sc = sc + wi[:, h:h + 1] * jnp.maximum(d_all[h * qb:(h + 1) * qb, :], 0.0)
        sc = sc * IDX_SCALE
        sc = jnp.where(jnp.abs(sc) < MIN_NORMAL, 0.0, sc)
        adm = b * kb + lane < vis_end
        keys_scr[b] = jnp.where(adm, _score_key(sc), INT_MIN)
        smax, smin = carry
        hi_part = jnp.where(adm, sc, NEG_INF)
        lo_part = jnp.where(adm, sc, -NEG_INF)
        for t in range(kb // LANES):
            smax = jnp.maximum(smax, hi_part[:, t * LANES:(t + 1) * LANES])
            smin = jnp.minimum(smin, lo_part[:, t * LANES:(t + 1) * LANES])
        return smax, smin

    smax, smin = lax.fori_loop(0, nkb, score_block,
                               (jnp.full((qb, LANES), NEG_INF, f32), jnp.full((qb, LANES), -NEG_INF, f32)))
    smax = jnp.max(smax, axis=1, keepdims=True)
    smin = jnp.min(smin, axis=1, keepdims=True)

    def count(pred):
        def body(b, acc):
            for t in range(kb // LANES):
                hit = pred(keys_scr[b, :, t * LANES:(t + 1) * LANES], b * kb + t * LANES)
                acc = acc + jnp.where(hit, 1.0, 0.0)
            return acc
        acc = lax.fori_loop(0, nkb, body, jnp.zeros((qb, LANES), f32))
        return jnp.sum(acc, axis=1, keepdims=True)

    def wide(x):
        return jnp.broadcast_to(x, (qb, LANES))

    nsel = jnp.float32(n_sel)

    def unsettled(ct):
        return jnp.max(jnp.abs(ct - nsel)) > 0.0

    def search_cond(carry):
        t, lo, hi, clo = carry[:4]
        open_rows = jnp.where((clo == nsel) | (hi == lo + 1), 0.0, 1.0)
        return jnp.max(open_rows) > 0.0

    def search_one(t, lo, hi, clo, glo, ghi, side):
        vlo, vhi = _key_score(lo), _key_score(hi - 1)
        frac = glo / (glo - ghi)
        cand = _score_key(vlo + (vhi - vlo) * frac)
        cand = jnp.where(t >= SEARCH_FALLBACK, lo + jnp.right_shift(hi - lo, 1), cand)
        cand = jnp.where((t == 0) & (lo < 0) & (hi > 0), 0, cand)
        cand = jnp.where((t == 1) & (lo <= 0) & (hi > 1), 1, cand)
        cand = jnp.maximum(lo + 1, jnp.minimum(cand, hi - 1))
        cw = wide(cand)
        c = count(lambda k, c0: k >= cw)
        ok = c >= nsel
        g = jnp.log(jnp.maximum(c, 0.5) / nsel)
        ghi = jnp.where(ok & (side > 0), 0.5 * ghi, ghi)
        glo = jnp.where(jnp.logical_not(ok) & (side < 0), 0.5 * glo, glo)
        return (jnp.where(ok, cand, lo), jnp.where(ok, hi, cand), jnp.where(ok, c, clo),
                jnp.where(ok, g, glo), jnp.where(ok, ghi, g), jnp.where(ok, 1, -1))

    def search_step(carry):
        t, state = carry[0], carry[1:]
        state = search_one(t, *state)
        state = search_one(t + 1, *state)
        return (t + 2,) + state

    nvis = vis_end.astype(f32)
    short = nvis <= nsel
    lo0 = jnp.where(short, INT_MIN, _score_key(smin))
    hi0 = jnp.where(short, INT_MIN + 1, _score_key(smax) + 1)
    clo0 = jnp.where(short, UNKNOWN_COUNT, nvis)
    glo0 = jnp.log(jnp.maximum(nvis, nsel) / nsel)
    ghi0 = jnp.full((qb, 1), np.log(0.5 / n_sel), f32)
    _, tau, _, ctau, _, _, _ = lax.while_loop(
        search_cond, search_step, (jnp.int32(0), lo0, hi0, clo0, glo0, ghi0, jnp.zeros((qb, 1), i32)))

    @pl.when(unsettled(ctau))
    def _():
        tw = wide(tau)
        need = nsel - count(lambda k, c0: k > tw)
        need = jnp.where(tau == INT_MIN, UNKNOWN_COUNT, need)

        def demote(b, seen):
            k = keys_scr[b]
            eq = k == tau
            before = _dot(jnp.where(eq, 1.0, 0.0).astype(MXU_DTYPE), upper_scr[...]) + seen
            keys_scr[b] = jnp.where(eq & (before >= need), tau - 1, k)
            return seen + jnp.sum(jnp.where(eq, 1.0, 0.0), axis=1, keepdims=True)

        lax.fori_loop(0, nkb, demote, jnp.zeros((qb, 1), f32))

    tau_sel = jnp.maximum(tau, INT_MIN + 1)

    q = q_ref[...]
    rq = lax.broadcasted_iota(i32, (GROUP * qb, qb), 0) % qb
    onehot = jnp.where(rq == lax.broadcasted_iota(i32, (GROUP * qb, qb), 1), 1.0, 0.0).astype(MXU_DTYPE)
    qaug = [jnp.concatenate([jnp.concatenate([q[:, (g * GROUP + j) * HEAD_DIM:(g * GROUP + j + 1) * HEAD_DIM]
                                              for j in range(GROUP)], axis=0), onehot], axis=1)
            for g in range(N_KV_HEADS)]
    m_scr[...] = jnp.full(m_scr.shape, MAX_FLOOR, f32)
    acc_scr[...] = jnp.zeros(acc_scr.shape, f32)
    c2 = HEAD_DIM ** -0.5 * LOG2_E

    def logits(b, s_ref, smax_ref):
        k = keys_scr[b]
        mask_rows = jnp.where(k >= tau_sel, 0.0, MASKED_LOGIT).astype(MXU_DTYPE)
        for g in range(N_KV_HEADS):
            kaug_scr[g, :HEAD_DIM, :] = kt_ref[b, g * HEAD_DIM:(g + 1) * HEAD_DIM, :]
            kaug_scr[g, HEAD_DIM:, :] = mask_rows
        for g in range(N_KV_HEADS):
            s = _dot(qaug[g], kaug_scr[g]) * c2
            s_ref[g] = s
            smax_ref[g] = jnp.max(s, axis=1, keepdims=True)

    def update(b, s_ref, smax_ref):
        for g in range(N_KV_HEADS):
            m_old = m_scr[g]
            m_new = jnp.maximum(m_old, smax_ref[g])
            p = jnp.exp2(s_ref[g] - m_new)
            alpha = jnp.exp2(m_old - m_new)
            vaug_scr[g, :, :HEAD_DIM] = v_ref[b, :, g * HEAD_DIM:(g + 1) * HEAD_DIM]
            pv = _dot(p.astype(MXU_DTYPE), vaug_scr[g])
            acc_scr[g] = alpha * acc_scr[g] + pv
            m_scr[g] = m_new

    logits(0, sa_scr, ma_scr)

    def block_pair(ip, carry):
        b0 = 2 * ip
        logits(jnp.minimum(b0 + 1, nkb - 1), sb_scr, mb_scr)
        update(b0, sa_scr, ma_scr)

        @pl.when(b0 + 1 < nkb)
        def _():
            logits(jnp.minimum(b0 + 2, nkb - 1), sa_scr, ma_scr)
            update(b0 + 1, sb_scr, mb_scr)
        return carry

    lax.fori_loop(0, (nkb + 1) // 2, block_pair, 0)

    for g in range(N_KV_HEADS):
        o = acc_scr[g, :, :HEAD_DIM] / acc_scr[g, :, HEAD_DIM:HEAD_DIM + 1]
        for j in range(GROUP):
            h = g * GROUP + j
            o_ref[:, h * HEAD_DIM:(h + 1) * HEAD_DIM] = o[j * qb:(j + 1) * qb, :]


def _dsa_attention(q, qi, wi, ki_all, k_all, v_all, *, qb, kb, s_true, pos0):
    bsz, t, _ = q.shape
    s_pad = k_all.shape[1]
    nk = s_pad // kb
    n_sel = min(TOPK_KEYS, s_true // 4)
    kit = ki_all.reshape(bsz, nk, kb, IDX_DIM).transpose(0, 1, 3, 2)
    kt = k_all.reshape(bsz, nk, kb, N_KV_HEADS * HEAD_DIM).transpose(0, 1, 3, 2)
    vv = v_all.reshape(bsz, nk, kb, N_KV_HEADS * HEAD_DIM)
    qrow = lambda b, i: (b, i, 0)
    kmap = lambda b, i: (b, 0, 0, 0)
    kern = functools.partial(_dsa_kernel, qb=qb, kb=kb, s_true=s_true, pos0=pos0, n_sel=n_sel)
    return pl.pallas_call(
        kern,
        grid=(bsz, t // qb),
        in_specs=[pl.BlockSpec((None, qb, q.shape[2]), qrow),
                  pl.BlockSpec((None, qb, qi.shape[2]), qrow),
                  pl.BlockSpec((None, qb, wi.shape[2]), qrow),
                  _resident((None, nk, IDX_DIM, kb), kmap),
                  _resident((None, nk, N_KV_HEADS * HEAD_DIM, kb), kmap),
                  _resident((None, nk, kb, N_KV_HEADS * HEAD_DIM), kmap)],
        out_specs=pl.BlockSpec((None, qb, q.shape[2]), qrow),
        out_shape=jax.ShapeDtypeStruct(q.shape, f32),
        scratch_shapes=[pltpu.VMEM((nk, qb, kb), i32),
                        pltpu.VMEM((N_KV_HEADS, HEAD_DIM + qb, kb), MXU_DTYPE),
                        pltpu.VMEM((N_KV_HEADS, GROUP * qb, kb), f32),
                        pltpu.VMEM((N_KV_HEADS, GROUP * qb, kb), f32),
                        pltpu.VMEM((N_KV_HEADS, GROUP * qb, 1), f32),
                        pltpu.VMEM((N_KV_HEADS, GROUP * qb, 1), f32),
                        pltpu.VMEM((N_KV_HEADS, kb, 2 * HEAD_DIM), MXU_DTYPE),
                        pltpu.VMEM((N_KV_HEADS, GROUP * qb, 1), f32),
                        pltpu.VMEM((N_KV_HEADS, GROUP * qb, 2 * HEAD_DIM), f32),
                        pltpu.VMEM((kb, kb), MXU_DTYPE)],
        compiler_params=_cparams(("parallel", "arbitrary")),
        name="dsa_attention",
    )(q, qi, wi, kit, kt, vv)


def _expm1(y):
    u = jnp.exp(y)
    um1 = u - 1.0
    return jnp.where(um1 == 0.0, y, jnp.where(um1 == -1.0, -1.0, um1 * y / jnp.log(u)))


def _rglru_kernel(xr_ref, gr_ref, cs_ref, h0_ref, wc_ref, bc_ref, wax_ref, ba_ref, bx_ref, lam_ref,
                  rnn_ref, hlast_ref, xp_scr, a_scr, b_scr, h_scr, *, reset_first):
    t = pl.program_id(1)
    tt, c = xr_ref.shape
    lead = SUBLANES

    @pl.when(t == 0)
    def _():
        xp_scr[lead - (CONV_W - 1):lead, :] = cs_ref[...]
        h_scr[...] = h0_ref[...]

    xp_scr[lead:lead + tt, :] = xr_ref[...]
    conv = bc_ref[...]
    for j in range(CONV_W):
        off = lead - (CONV_W - 1) + j
        conv = conv + xp_scr[off:off + tt, :] * wc_ref[j:j + 1, :]
    xp_scr[lead - (CONV_W - 1):lead, :] = xp_scr[lead + tt - (CONV_W - 1):lead + tt, :]

    cb = conv.astype(MXU_DTYPE)
    sp = lam_ref[...]
    grow = t * tt + lax.broadcasted_iota(i32, (tt, 1), 0)
    for g in range(c // RNN_GROUP):
        sl = slice(g * RNN_GROUP, (g + 1) * RNN_GROUP)
        z = _dot(cb[:, sl], wax_ref[g])
        r = jax.nn.sigmoid(z[:, :RNN_GROUP] + ba_ref[:, sl])
        ig = jax.nn.sigmoid(z[:, RNN_GROUP:] + bx_ref[:, sl])
        log_a = -LRU_C * r * sp[:, sl]
        mult = jnp.sqrt(-_expm1(2.0 * log_a))
        if reset_first:
            mult = jnp.where(grow == 0, 1.0, mult)
        a_scr[:, sl] = jnp.exp(log_a)
        b_scr[:, sl] = mult * (ig * conv[:, sl])

    def rows(jb, h):
        base = pl.multiple_of(jb * SUBLANES, SUBLANES)
        for u in range(SUBLANES):
            h = a_scr[pl.ds(base + u, 1), :] * h + b_scr[pl.ds(base + u, 1), :]
            rnn_ref[pl.ds(base + u, 1), :] = h
        return h

    h = lax.fori_loop(0, tt // SUBLANES, rows, h_scr[...])
    h_scr[...] = h
    rnn_ref[...] = rnn_ref[...] * jax.nn.gelu(gr_ref[...])

    @pl.when(t == pl.num_programs(1) - 1)
    def _():
        hlast_ref[...] = h


def _blockdiag_groups(w):
    per = RNN_GROUP // RNN_BLOCK
    g = w.reshape(N_RNN_BLOCKS // per, per, RNN_BLOCK, RNN_BLOCK)
    eye = jnp.eye(per, dtype=w.dtype)
    return jnp.einsum('gacd,ab->gacbd', g, eye).reshape(N_RNN_BLOCKS // per, RNN_GROUP, RNN_GROUP)


def _rglru(xr, gr, conv_state, h0, w_conv, b_conv, w_a, b_a, w_x, b_x, lam, *, tt, reset_first):
    bsz, t, c = xr.shape
    wax = jnp.concatenate([_blockdiag_groups(w_a), _blockdiag_groups(w_x)], axis=2).astype(MXU_DTYPE)
    sp = jax.nn.softplus(-lam.astype(f32)).reshape(1, c)
    seq = lambda b, i: (b, i, 0)
    per_b = lambda b, i: (b, 0, 0)
    const2 = lambda b, i: (0, 0)
    const3 = lambda b, i: (0, 0, 0)
    kern = functools.partial(_rglru_kernel, reset_first=reset_first)
    return pl.pallas_call(
        kern,
        grid=(bsz, t // tt),
        in_specs=[pl.BlockSpec((None, tt, c), seq), pl.BlockSpec((None, tt, c), seq),
                  pl.BlockSpec((None, CONV_W - 1, c), per_b), pl.BlockSpec((None, 1, c), per_b),
                  pl.BlockSpec((CONV_W, c), const2), pl.BlockSpec((1, c), const2),
                  pl.BlockSpec(wax.shape, const3), pl.BlockSpec((1, c), const2),
                  pl.BlockSpec((1, c), const2), pl.BlockSpec((1, c), const2)],
        out_specs=[pl.BlockSpec((None, tt, c), seq), pl.BlockSpec((None, 1, c), per_b)],
        out_shape=[jax.ShapeDtypeStruct((bsz, t, c), f32), jax.ShapeDtypeStruct((bsz, 1, c), f32)],
        scratch_shapes=[pltpu.VMEM((tt + SUBLANES, c), f32), pltpu.VMEM((tt, c), f32),
                        pltpu.VMEM((tt, c), f32), pltpu.VMEM((1, c), f32)],
        compiler_params=_cparams(("parallel", "arbitrary")),
        name="rglru",
    )(xr, gr, conv_state, h0.reshape(bsz, 1, c), w_conv, b_conv.reshape(1, c), wax,
      b_a.reshape(1, c), b_x.reshape(1, c), sp)


def _layer_norm(z, g, b):
    mu = jnp.mean(z, axis=-1, keepdims=True)
    var = jnp.mean(jnp.square(z - mu), axis=-1, keepdims=True)
    return (z - mu) * lax.rsqrt(var + LN_EPS) * g + b


def _merge_kernel(attn_ref, rnn_ref, ga_ref, gb_ref, x_ref, wo_ref, g1_ref, b1_ref, wr_ref, br_ref,
                  x1_ref, x1b_ref, cw_ref, *, dn_alpha):
    merged = jax.nn.sigmoid(ga_ref[...]) * attn_ref[...] + jax.nn.sigmoid(gb_ref[...]) * rnn_ref[...]
    m = _dot(merged.astype(MXU_DTYPE), wo_ref[...])
    x1 = _layer_norm(dn_alpha * x_ref[...] + m, g1_ref[...], b1_ref[...])
    x1_ref[...] = x1
    x1b = x1.astype(MXU_DTYPE)
    x1b_ref[...] = x1b
    logits = _dot(x1b, wr_ref[...]) + br_ref[...]
    ne = logits.shape[1]
    eidx = lax.broadcasted_iota(i32, logits.shape, 1).astype(f32)
    work = logits
    chosen = jnp.zeros(logits.shape, jnp.bool_)
    top = None
    for kk in range(TOP_K):
        mx = jnp.max(work, axis=1, keepdims=True)
        if kk == 0:
            top = mx
        first = jnp.min(jnp.where(work == mx, eidx, ne), axis=1, keepdims=True)
        pick = eidx == first
        chosen = chosen | pick
        work = jnp.where(pick, NEG_INF, work)
    e = jnp.where(chosen, jnp.exp(logits - top), 0.0)
    cw_ref[...] = e / jnp.sum(e, axis=1, keepdims=True)


def _merge_ln_router(attn, rnn, ga, gb, x2d, w_out, ln_g, ln_b, w_router, b_router, *, tm, dn_alpha):
    n, d = x2d.shape
    ne = w_router.shape[1]
    row = lambda i: (i, 0)
    const = lambda i: (0, 0)
    kern = functools.partial(_merge_kernel, dn_alpha=dn_alpha)
    return pl.pallas_call(
        kern,
        grid=(n // tm,),
        in_specs=[pl.BlockSpec((tm, d), row)] * 5
                 + [pl.BlockSpec((d, d), const), pl.BlockSpec((1, d), const), pl.BlockSpec((1, d), const),
                    pl.BlockSpec((d, ne), const), pl.BlockSpec((1, ne), const)],
        out_specs=[pl.BlockSpec((tm, d), row), pl.BlockSpec((tm, d), row), pl.BlockSpec((tm, ne), row)],
        out_shape=[jax.ShapeDtypeStruct((n, d), f32), jax.ShapeDtypeStruct((n, d), MXU_DTYPE),
                   jax.ShapeDtypeStruct((n, ne), f32)],
        compiler_params=_cparams(("parallel",)),
        name="merge_ln_router",
    )(attn, rnn, ga, gb, x2d, w_out.astype(MXU_DTYPE), ln_g.reshape(1, d), ln_b.reshape(1, d),
      w_router.astype(MXU_DTYPE), b_router.reshape(1, ne))


MOE_CHUNK = 128


def _moe_kernel(xb_ref, cwt_ref, wg_ref, bg_ref, wu_ref, bu_ref, wd_ref, bd_ref, o_ref, rankt_scr):
    e = pl.program_id(1)
    ne, tm = cwt_ref.shape

    @pl.when(e == 0)
    def _():
        selt = jnp.where(cwt_ref[...] > 0.0, 1.0, 0.0)
        upper = jnp.where(lax.broadcasted_iota(i32, (tm, tm), 0) < lax.broadcasted_iota(i32, (tm, tm), 1),
                          1.0, 0.0).astype(MXU_DTYPE)
        rankt = _dot(selt.astype(MXU_DTYPE), upper)
        rankt_scr[...] = jnp.where(selt > 0.0, rankt, -1.0)
        o_ref[...] = jnp.zeros(o_ref.shape, f32)

    rrow = rankt_scr[pl.ds(e, 1), :]
    grow = cwt_ref[pl.ds(e, 1), :]
    cnt = jnp.max(rrow).astype(i32) + 1

    def run_chunk(first_rank, size):
        lo = first_rank.astype(f32)
        hit = rrow - lo == lax.broadcasted_iota(i32, (size, tm), 0).astype(f32)
        gather = jnp.where(hit, 1.0, 0.0).astype(MXU_DTYPE)
        gcomp = jnp.sum(jnp.where(hit, grow, 0.0), axis=1, keepdims=True)
        xg = _dot(gather, xb_ref[...]).astype(MXU_DTYPE)
        g = _dot(xg, wg_ref[...]) + bg_ref[...]
        u = _dot(xg, wu_ref[...]) + bu_ref[...]
        g = jnp.minimum(g, SWIGLU_LIMIT)
        u = jnp.clip(u, -SWIGLU_LIMIT, SWIGLU_LIMIT)
        hmid = g * jax.nn.sigmoid(SWIGLU_ALPHA * g) * (u + 1.0)
        y = (_dot(hmid.astype(MXU_DTYPE), wd_ref[...]) + bd_ref[...]) * gcomp
        y_hi = y.astype(MXU_DTYPE)
        y_lo = (y - y_hi.astype(f32)).astype(MXU_DTYPE)
        o_ref[...] += lax.dot_general(jnp.concatenate([gather, gather], axis=0),
                                      jnp.concatenate([y_hi, y_lo], axis=0),
                                      (((0,), (0,)), ((), ())), preferred_element_type=f32)

    def chunk(cidx, carry):
        run_chunk(cidx * MOE_CHUNK, MOE_CHUNK)
        return carry

    lax.fori_loop(0, (cnt + MOE_CHUNK - 1) // MOE_CHUNK, chunk, 0)


def _moe(x1b, cw, w_gate, b_gate, w_up, b_up, w_down, b_down, *, tm):
    n, d = x1b.shape
    ne = cw.shape[1]
    dff = w_gate.shape[2]
    row = lambda i, e: (i, 0)
    wmap = lambda i, e: (e, 0, 0)
    return pl.pallas_call(
        _moe_kernel,
        grid=(n // tm, ne),
        in_specs=[pl.BlockSpec((tm, d), row), pl.BlockSpec((ne, tm), lambda i, e: (0, i)),
                  pl.BlockSpec((None, d, dff), wmap), pl.BlockSpec((None, 1, dff), wmap),
                  pl.BlockSpec((None, d, dff), wmap), pl.BlockSpec((None, 1, dff), wmap),
                  pl.BlockSpec((None, dff, d), wmap), pl.BlockSpec((None, 1, d), wmap)],
        out_specs=pl.BlockSpec((tm, d), row),
        out_shape=jax.ShapeDtypeStruct((n, d), f32),
        scratch_shapes=[pltpu.VMEM((ne, tm), f32)],
        compiler_params=_cparams(("parallel", "arbitrary")),
        name="moe_experts",
    )(x1b, cw.T, w_gate.astype(MXU_DTYPE), b_gate.reshape(ne, 1, dff), w_up.astype(MXU_DTYPE),
      b_up.reshape(ne, 1, dff), w_down.astype(MXU_DTYPE), b_down.reshape(ne, 1, d))


def _final_kernel(x1_ref, ffn_ref, p_ref, g2_ref, b2_ref, wpg_ref, wpp_ref, y_ref, *, dn_alpha):
    x2 = _layer_norm(dn_alpha * x1_ref[...] + ffn_ref[...], g2_ref[...], b2_ref[...])
    gate = jax.nn.sigmoid(_dot(x2.astype(MXU_DTYPE), wpg_ref[...]))
    emb = _dot(p_ref[...].astype(MXU_DTYPE), wpp_ref[...])
    y_ref[...] = x2 + gate * emb


def _final(x1, ffn, p2d, ln_g, ln_b, w_ple_gate, w_ple_proj, *, tm, dn_alpha):
    n, d = x1.shape
    dp = p2d.shape[1]
    row = lambda i: (i, 0)
    const = lambda i: (0, 0)
    kern = functools.partial(_final_kernel, dn_alpha=dn_alpha)
    return pl.pallas_call(
        kern,
        grid=(n // tm,),
        in_specs=[pl.BlockSpec((tm, d), row), pl.BlockSpec((tm, d), row), pl.BlockSpec((tm, dp), row),
                  pl.BlockSpec((1, d), const), pl.BlockSpec((1, d), const),
                  pl.BlockSpec((d, d), const), pl.BlockSpec((dp, d), const)],
        out_specs=pl.BlockSpec((tm, d), row),
        out_shape=jax.ShapeDtypeStruct((n, d), f32),
        compiler_params=_cparams(("parallel",)),
        name="ln2_ple",
    )(x1, ffn, p2d, ln_g.reshape(1, d), ln_b.reshape(1, d), w_ple_gate.astype(MXU_DTYPE),
      w_ple_proj.astype(MXU_DTYPE))


ROW_TILE = 512
TIME_TILE = 512
MOE_TILE = 1024
KEY_BLOCK = 1024
KEY_BLOCK_SMALL = 3 * LANES


def _pick_tile(n, pref):
    t = min(pref, n)
    while n % t:
        t //= 2
    return t


def _layer(x, p, pos0, past_k, past_v, past_ki, conv_state, h0, prm, depth):
    (w_in, w_conv, b_conv, w_a, b_a, w_x, b_x, lam, w_out, ln1_g, ln1_b, w_router, b_router,
     w_gate, b_gate, w_up, b_up, w_down, b_down, ln2_g, ln2_b, w_ple_gate, w_ple_proj) = prm
    bsz, t, d = x.shape
    n = bsz * t
    past = past_k.shape[1]
    dn_alpha = (2 * depth) ** 0.25

    tm = _pick_tile(n, ROW_TILE)
    pos = pos0 + jnp.arange(t)
    tabs_a = _rope_tables(pos, ROPE_DIM, HEAD_DIM)
    tabs_i = _rope_tables(pos, IDX_ROPE_DIM, IDX_DIM)
    if t % tm:
        rep = tm // t
        tabs_a = tuple(jnp.tile(a, (rep, 1)) for a in tabs_a)
        tabs_i = tuple(jnp.tile(a, (rep, 1)) for a in tabs_i)
    (q, k, kb, v, vb, qi, ki, kib, wi, xr, gr, ga, gb) = _in_projection(
        x.reshape(n, d), _pack_w_in(w_in), tabs_a, tabs_i, tm)

    s_true = past + t
    qb = min(Q_BLOCK, t)
    kblk = KEY_BLOCK if s_true % KEY_BLOCK == 0 else KEY_BLOCK_SMALL
    s_pad = -(-s_true // kblk) * kblk

    def keys_by_position(past_arr, new_arr):
        new_arr = new_arr.reshape(bsz, t, -1)
        parts = [new_arr]
        if past:
            parts.insert(0, past_arr.reshape(bsz, past, new_arr.shape[2]).astype(MXU_DTYPE))
        if s_pad > s_true:
            parts.append(jnp.zeros((bsz, s_pad - s_true, new_arr.shape[2]), MXU_DTYPE))
        return jnp.concatenate(parts, axis=1)

    attn = _dsa_attention(q.reshape(bsz, t, -1), qi.reshape(bsz, t, -1), wi.reshape(bsz, t, -1),
                          keys_by_position(past_ki, kib), keys_by_position(past_k, kb),
                          keys_by_position(past_v, vb), qb=qb, kb=kblk, s_true=s_true, pos0=pos0)

    tt = _pick_tile(t, TIME_TILE)
    xr3 = xr.reshape(bsz, t, -1)
    rnn, h_last = _rglru(xr3, gr.reshape(bsz, t, -1), conv_state, h0, w_conv, b_conv, w_a, b_a, w_x, b_x,
                         lam, tt=tt, reset_first=(pos0 == 0))
    new_conv = jnp.concatenate([conv_state, xr3], axis=1)[:, -(CONV_W - 1):]

    x1, x1b, cw = _merge_ln_router(attn.reshape(n, d), rnn.reshape(n, d), ga, gb, x.reshape(n, d), w_out,
                                   ln1_g, ln1_b, w_router, b_router, tm=tm, dn_alpha=dn_alpha)
    ffn = _moe(x1b, cw, w_gate, b_gate, w_up, b_up, w_down, b_down, tm=_pick_tile(n, MOE_TILE))
    y = _final(x1, ffn, p.reshape(n, -1), ln2_g, ln2_b, w_ple_gate, w_ple_proj, tm=tm, dn_alpha=dn_alpha)

    return (y.reshape(bsz, t, d), k.reshape(bsz, t, N_KV_HEADS, HEAD_DIM), v.reshape(bsz, t, N_KV_HEADS, HEAD_DIM),
            ki.reshape(bsz, t, IDX_DIM), new_conv, h_last.reshape(bsz, -1))


def kernel(x_prompt, x_sample, cache_k, cache_v, cache_kidx, state_conv, state_h, p_prompt, p_sample,
           w_in, w_conv, b_conv, w_a, b_a, w_x, b_x, lru_lambda, w_out, ln1_g, ln1_b, w_router, b_router,
           w_gate, b_gate, w_up, b_up, w_down, b_down, ln2_g, ln2_b, w_ple_gate, w_ple_proj):
    depth = w_in.shape[0]
    bp = x_prompt.shape[0]
    past_len = cache_k.shape[2]
    dt = x_prompt.dtype
    empty_kv = jnp.zeros((bp, 0, N_KV_HEADS, HEAD_DIM), dt)
    empty_ki = jnp.zeros((bp, 0, IDX_DIM), dt)
    zero_conv = jnp.zeros((bp, CONV_W - 1, x_prompt.shape[2]), dt)
    zero_h = jnp.zeros((bp, x_prompt.shape[2]), dt)

    yp, ys = x_prompt, x_sample
    outs_p, outs_s = [], []
    for l in range(depth):
        prm = (w_in[l], w_conv[l], b_conv[l], w_a[l], b_a[l], w_x[l], b_x[l], lru_lambda[l], w_out[l],
               ln1_g[l], ln1_b[l], w_router[l], b_router[l], w_gate[l], b_gate[l], w_up[l], b_up[l],
               w_down[l], b_down[l], ln2_g[l], ln2_b[l], w_ple_gate[l], w_ple_proj[l])
        yp, *rest_p = _layer(yp, p_prompt[l], 0, empty_kv, empty_kv, empty_ki, zero_conv, zero_h, prm, depth)
        ys, *rest_s = _layer(ys, p_sample[l], past_len, cache_k[l], cache_v[l], cache_kidx[l],
                             state_conv[l], state_h[l], prm, depth)
        outs_p.append(rest_p)
        outs_s.append(rest_s)

    stack = lambda outs, j: jnp.stack([o[j] for o in outs])
    return (yp, ys) + tuple(stack(outs_p, j) for j in range(5)) + tuple(stack(outs_s, j) for j in range(5))
```

```python
import functools

import jax
import jax.numpy as jnp
import numpy as np
from jax import lax
from jax.experimental import pallas as pl
from jax.experimental.pallas import tpu as pltpu

f32 = jnp.float32
i32 = jnp.int32
MXU_DTYPE = jnp.bfloat16

CHUNK = 64
HEAD_DIM = 128
N_KV_HEADS = 2
GROUP = 4
ROPE_DIM = 32
ROPE_THETA = 500000.0
N_IDX_HEADS = 8
IDX_DIM = 64
IDX_ROPE_DIM = 16
IDX_SCALE = (IDX_DIM * N_IDX_HEADS) ** -0.5
TOPK_KEYS = 256
Q_BLOCK = 128
N_RNN_BLOCKS = 16
RNN_BLOCK = 64
RNN_GROUP = 256
CONV_W = 4
LRU_C = 8.0
N_EXPERTS = 32
TOP_K = 4
SWIGLU_LIMIT = 7.0
SWIGLU_ALPHA = 1.702
LN_EPS = 1e-5

LANES = 128
SUBLANES = 8
VMEM_LIMIT = 56 * 1024 * 1024
INT_MIN = -2 ** 31
NEG_INF = float("-inf")
LOG2_E = 1.4426950408889634


def _cparams(sem):
    return pltpu.CompilerParams(dimension_semantics=sem, vmem_limit_bytes=VMEM_LIMIT)


def _resident(shape, index_map):
    return pl.BlockSpec(shape, index_map, pipeline_mode=pl.Buffered(1))


def _dot(a, b):
    return jnp.dot(a, b, preferred_element_type=f32)


D_MODEL = N_KV_HEADS * GROUP * HEAD_DIM
D_KV = N_KV_HEADS * HEAD_DIM
D_QI = N_IDX_HEADS * IDX_DIM
_SEG_WIDTHS = (D_MODEL, D_KV, D_KV, D_QI, LANES, D_MODEL, D_MODEL, D_MODEL, D_MODEL)
_C_Q, _C_K, _C_V, _C_QI, _C_KW, _C_XR, _C_GR, _C_GA, _C_GB, _C_END = (
    int(c) for c in np.concatenate([[0], np.cumsum(_SEG_WIDTHS)]))


def _rope_tile(h, c, s_lo, s_hi, half):
    return h * c + pltpu.roll(h, half, 1) * s_hi + pltpu.roll(h, LANES - half, 1) * s_lo


def _inproj_kernel(x_ref, w_ref, ca_ref, sla_ref, sha_ref, ci_ref, sli_ref, shi_ref,
                   q_ref, k_ref, kb_ref, v_ref, vb_ref, qi_ref, ki_ref, kib_ref, wi_ref,
                   xr_ref, gr_ref, ga_ref, gb_ref):
    xb = x_ref[...].astype(MXU_DTYPE)

    def proj(a, b):
        return _dot(xb, w_ref[:, a:b])

    ca, sla, sha = ca_ref[...], sla_ref[...], sha_ref[...]
    ci, sli, shi = ci_ref[...], sli_ref[...], shi_ref[...]

    hq = proj(_C_Q, _C_K)
    for j in range(hq.shape[1] // LANES):
        t = _rope_tile(hq[:, j * LANES:(j + 1) * LANES], ca, sla, sha, ROPE_DIM // 2)
        q_ref[:, j * LANES:(j + 1) * LANES] = t.astype(q_ref.dtype)
    hk = proj(_C_K, _C_V)
    for j in range(hk.shape[1] // LANES):
        t = _rope_tile(hk[:, j * LANES:(j + 1) * LANES], ca, sla, sha, ROPE_DIM // 2)
        k_ref[:, j * LANES:(j + 1) * LANES] = t
        kb_ref[:, j * LANES:(j + 1) * LANES] = t.astype(kb_ref.dtype)
    hv = proj(_C_V, _C_QI)
    v_ref[...] = hv
    vb_ref[...] = hv.astype(vb_ref.dtype)
    hqi = proj(_C_QI, _C_KW)
    for j in range(hqi.shape[1] // LANES):
        t = _rope_tile(hqi[:, j * LANES:(j + 1) * LANES], ci, sli, shi, IDX_ROPE_DIM // 2)
        qi_ref[:, j * LANES:(j + 1) * LANES] = t.astype(qi_ref.dtype)
    hkw = proj(_C_KW, _C_XR)
    t = _rope_tile(hkw, ci, sli, shi, IDX_ROPE_DIM // 2)
    ki_ref[...] = t[:, :IDX_DIM]
    kib_ref[...] = t[:, :IDX_DIM].astype(kib_ref.dtype)
    wi_ref[...] = hkw[:, IDX_DIM:IDX_DIM + N_IDX_HEADS]
    xr_ref[...] = proj(_C_XR, _C_GR)
    gr_ref[...] = proj(_C_GR, _C_GA)
    ga_ref[...] = proj(_C_GA, _C_GB)
    gb_ref[...] = proj(_C_GB, _C_END)


def _rope_tables(pos, rot_dim, head_dim):
    half = rot_dim // 2
    inv = ROPE_THETA ** (-jnp.arange(half, dtype=f32) / half)
    ang = pos.astype(f32)[:, None] * inv[None, :]
    cos, sin = jnp.cos(ang), jnp.sin(ang)
    n = pos.shape[0]
    one = jnp.ones((n, head_dim - rot_dim), f32)
    zero_h = jnp.zeros((n, half), f32)
    zero_r = jnp.zeros((n, head_dim - rot_dim), f32)
    c = jnp.concatenate([cos, cos, one], axis=1)
    s_lo = jnp.concatenate([-sin, zero_h, zero_r], axis=1)
    s_hi = jnp.concatenate([zero_h, sin, zero_r], axis=1)
    rep = LANES // head_dim
    return tuple(jnp.tile(t, (1, rep)) for t in (c, s_lo, s_hi))


def _pack_w_in(w_in):
    d = w_in.shape[0]
    pad = jnp.zeros((d, LANES - IDX_DIM - N_IDX_HEADS), w_in.dtype)
    split = _C_KW + IDX_DIM + N_IDX_HEADS
    return jnp.concatenate([w_in[:, :split], pad, w_in[:, split:]], axis=1).astype(MXU_DTYPE)


def _in_projection(x2d, w_packed, tabs_a, tabs_i, tm):
    n, d = x2d.shape
    nt = tabs_a[0].shape[0] // tm
    row = lambda i: (i, 0)
    tab = lambda i: (i % nt, 0)
    out_cols = [(D_MODEL, MXU_DTYPE), (D_KV, f32), (D_KV, MXU_DTYPE), (D_KV, f32), (D_KV, MXU_DTYPE),
                (D_QI, MXU_DTYPE), (IDX_DIM, f32), (IDX_DIM, MXU_DTYPE), (N_IDX_HEADS, f32),
                (D_MODEL, f32), (D_MODEL, f32), (D_MODEL, f32), (D_MODEL, f32)]
    return pl.pallas_call(
        _inproj_kernel,
        grid=(n // tm,),
        in_specs=[pl.BlockSpec((tm, d), row), _resident(w_packed.shape, lambda i: (0, 0))]
                 + [pl.BlockSpec((tm, LANES), tab)] * 6,
        out_specs=[pl.BlockSpec((tm, c), row) for c, _ in out_cols],
        out_shape=[jax.ShapeDtypeStruct((n, c), dt) for c, dt in out_cols],
        compiler_params=_cparams(("parallel",)),
        name="in_projection",
    )(x2d, w_packed, *tabs_a, *tabs_i)


MASKED_LOGIT = -2.0 ** 126
MAX_FLOOR = -2.0 ** 120
UNKNOWN_COUNT = 1e9
MIN_NORMAL = 2.0 ** -126


SEARCH_FALLBACK = 40


def _score_key(s):
    bits = pltpu.bitcast(s, i32)
    return bits ^ ((bits >> 31) & 0x7FFFFFFF)


def _key_score(k):
    return pltpu.bitcast(k ^ ((k >> 31) & 0x7FFFFFFF), f32)


def _dsa_kernel(q_ref, qi_ref, wi_ref, kit_ref, kt_ref, v_ref, o_ref,
                keys_scr, kaug_scr, sa_scr, sb_scr, ma_scr, mb_scr, vaug_scr, m_scr, acc_scr, upper_scr,
                *, qb, kb, s_true, pos0, n_sel):
    i = pl.program_id(1)

    @pl.when(i == 0)
    def _():
        upper_scr[...] = jnp.where(lax.broadcasted_iota(i32, (kb, kb), 0) < lax.broadcasted_iota(i32, (kb, kb), 1),
                                   1.0, 0.0).astype(MXU_DTYPE)
        ones_col = jnp.where(lax.broadcasted_iota(i32, (kb, HEAD_DIM), 1) == 0, 1.0, 0.0).astype(MXU_DTYPE)
        for g in range(N_KV_HEADS):
            vaug_scr[g, :, HEAD_DIM:] = ones_col

    row = lax.broadcasted_iota(i32, (qb, 1), 0)
    pos = pos0 + i * qb + row
    vis_end = jnp.minimum((pos // CHUNK + 1) * CHUNK, s_true)
    pos_last = pos0 + i * qb + (qb - 1)
    kend = jnp.minimum((pos_last // CHUNK + 1) * CHUNK, s_true)
    nkb = (kend + kb - 1) // kb
    lane = lax.broadcasted_iota(i32, (qb, kb), 1)

    qi = qi_ref[...]
    wi = wi_ref[...]
    qi_rows = jnp.concatenate([qi[:, h * IDX_DIM:(h + 1) * IDX_DIM] for h in range(N_IDX_HEADS)], axis=0)

    def score_block(b, carry):
        kit = kit_ref[b]
        sc = jnp.zeros((qb, kb), f32)
        d_all = _dot(qi_rows, kit)
        for h in range(N_IDX_HEADS):
            sc = sc + wi[:, h:h + 1] * jnp.maximum(d_all[h * qb:(h + 1) * qb, :], 0.0)
        sc = sc * IDX_SCALE
        sc = jnp.where(jnp.abs(sc) < MIN_NORMAL, 0.0, sc)
        adm = b * kb + lane < vis_end
        keys_scr[b] = jnp.where(adm, _score_key(sc), INT_MIN)
        smax, smin = carry
        hi_part = jnp.where(adm, sc, NEG_INF)
        lo_part = jnp.where(adm, sc, -NEG_INF)
        for t in range(kb // LANES):
            smax = jnp.maximum(smax, hi_part[:, t * LANES:(t + 1) * LANES])
            smin = jnp.minimum(smin, lo_part[:, t * LANES:(t + 1) * LANES])
        return smax, smin

    smax, smin = lax.fori_loop(0, nkb, score_block,
                               (jnp.full((qb, LANES), NEG_INF, f32), jnp.full((qb, LANES), -NEG_INF, f32)))
    smax = jnp.max(smax, axis=1, keepdims=True)
    smin = jnp.min(smin, axis=1, keepdims=True)

    def count(pred):
        def body(b, acc):
            for t in range(kb // LANES):
                hit = pred(keys_scr[b, :, t * LANES:(t + 1) * LANES], b * kb + t * LANES)
                acc = acc + jnp.where(hit, 1.0, 0.0)
            return acc
        acc = lax.fori_loop(0, nkb, body, jnp.zeros((qb, LANES), f32))
        return jnp.sum(acc, axis=1, keepdims=True)

    def wide(x):
        return jnp.broadcast_to(x, (qb, LANES))

    nsel = jnp.float32(n_sel)

    def unsettled(ct):
        return jnp.max(jnp.abs(ct - nsel)) > 0.0

    def search_cond(carry):
        t, lo, hi, clo = carry[:4]
        open_rows = jnp.where((clo == nsel) | (hi == lo + 1), 0.0, 1.0)
        return jnp.max(open_rows) > 0.0

    def search_one(t, lo, hi, clo, glo, ghi, side):
        vlo, vhi = _key_score(lo), _key_score(hi - 1)
        frac = glo / (glo - ghi)
        cand = _score_key(vlo + (vhi - vlo) * frac)
        cand = jnp.where(t >= SEARCH_FALLBACK, lo + jnp.right_shift(hi - lo, 1), cand)
        cand = jnp.where((t == 0) & (lo < 0) & (hi > 0), 0, cand)
        cand = jnp.where((t == 1) & (lo <= 0) & (hi > 1), 1, cand)
        cand = jnp.maximum(lo + 1, jnp.minimum(cand, hi - 1))
        cw = wide(cand)
        c = count(lambda k, c0: k >= cw)
        ok = c >= nsel
        g = jnp.log(jnp.maximum(c, 0.5) / nsel)
        ghi = jnp.where(ok & (side > 0), 0.5 * ghi, ghi)
        glo = jnp.where(jnp.logical_not(ok) & (side < 0), 0.5 * glo, glo)
        return (jnp.where(ok, cand, lo), jnp.where(ok, hi, cand), jnp.where(ok, c, clo),
                jnp.where(ok, g, glo), jnp.where(ok, ghi, g), jnp.where(ok, 1, -1))

    def search_step(carry):
        t, state = carry[0], carry[1:]
        state = search_one(t, *state)
        state = search_one(t + 1, *state)
        return (t + 2,) + state

    nvis = vis_end.astype(f32)
    short = nvis <= nsel
    lo0 = jnp.where(short, INT_MIN, _score_key(smin))
    hi0 = jnp.where(short, INT_MIN + 1, _score_key(smax) + 1)
    clo0 = jnp.where(short, UNKNOWN_COUNT, nvis)
    glo0 = jnp.log(jnp.maximum(nvis, nsel) / nsel)
    ghi0 = jnp.full((qb, 1), np.log(0.5 / n_sel), f32)
    _, tau, _, ctau, _, _, _ = lax.while_loop(
        search_cond, search_step, (jnp.int32(0), lo0, hi0, clo0, glo0, ghi0, jnp.zeros((qb, 1), i32)))

    @pl.when(unsettled(ctau))
    def _():
        tw = wide(tau)
        need = nsel - count(lambda k, c0: k > tw)
        need = jnp.where(tau == INT_MIN, UNKNOWN_COUNT, need)

        def demote(b, seen):
            k = keys_scr[b]
            eq = k == tau
            before = _dot(jnp.where(eq, 1.0, 0.0).astype(MXU_DTYPE), upper_scr[...]) + seen
            keys_scr[b] = jnp.where(eq & (before >= need), tau - 1, k)
            return seen + jnp.sum(jnp.where(eq, 1.0, 0.0), axis=1, keepdims=True)

        lax.fori_loop(0, nkb, demote, jnp.zeros((qb, 1), f32))

    tau_sel = jnp.maximum(tau, INT_MIN + 1)

    q = q_ref[...]
    rq = lax.broadcasted_iota(i32, (GROUP * qb, qb), 0) % qb
    onehot = jnp.where(rq == lax.broadcasted_iota(i32, (GROUP * qb, qb), 1), 1.0, 0.0).astype(MXU_DTYPE)
    qaug = [jnp.concatenate([jnp.concatenate([q[:, (g * GROUP + j) * HEAD_DIM:(g * GROUP + j + 1) * HEAD_DIM]
                                              for j in range(GROUP)], axis=0), onehot], axis=1)
            for g in range(N_KV_HEADS)]
    m_scr[...] = jnp.full(m_scr.shape, MAX_FLOOR, f32)
    acc_scr[...] = jnp.zeros(acc_scr.shape, f32)
    c2 = HEAD_DIM ** -0.5 * LOG2_E

    def logits(b, s_ref, smax_ref):
        k = keys_scr[b]
        mask_rows = jnp.where(k >= tau_sel, 0.0, MASKED_LOGIT).astype(MXU_DTYPE)
        for g in range(N_KV_HEADS):
            kaug_scr[g, :HEAD_DIM, :] = kt_ref[b, g * HEAD_DIM:(g + 1) * HEAD_DIM, :]
            kaug_scr[g, HEAD_DIM:, :] = mask_rows
        for g in range(N_KV_HEADS):
            s = _dot(qaug[g], kaug_scr[g]) * c2
            s_ref[g] = s
            smax_ref[g] = jnp.max(s, axis=1, keepdims=True)

    def update(b, s_ref, smax_ref):
        for g in range(N_KV_HEADS):
            m_old = m_scr[g]
            m_new = jnp.maximum(m_old, smax_ref[g])
            p = jnp.exp2(s_ref[g] - m_new)
            alpha = jnp.exp2(m_old - m_new)
            vaug_scr[g, :, :HEAD_DIM] = v_ref[b, :, g * HEAD_DIM:(g + 1) * HEAD_DIM]
            pv = _dot(p.astype(MXU_DTYPE), vaug_scr[g])
            acc_scr[g] = alpha * acc_scr[g] + pv
            m_scr[g] = m_new

    logits(0, sa_scr, ma_scr)

    def block_pair(ip, carry):
        b0 = 2 * ip
        logits(jnp.minimum(b0 + 1, nkb - 1), sb_scr, mb_scr)
        update(b0, sa_scr, ma_scr)

        @pl.when(b0 + 1 < nkb)
        def _():
            logits(jnp.minimum(b0 + 2, nkb - 1), sa_scr, ma_scr)
            update(b0 + 1, sb_scr, mb_scr)
        return carry

    lax.fori_loop(0, (nkb + 1) // 2, block_pair, 0)

    for g in range(N_KV_HEADS):
        o = acc_scr[g, :, :HEAD_DIM] / acc_scr[g, :, HEAD_DIM:HEAD_DIM + 1]
        for j in range(GROUP):
            h = g * GROUP + j
            o_ref[:, h * HEAD_DIM:(h + 1) * HEAD_DIM] = o[j * qb:(j + 1) * qb, :]


def _dsa_attention(q, qi, wi, ki_all, k_all, v_all, *, qb, kb, s_true, pos0):
    bsz, t, _ = q.shape
    s_pad = k_all.shape[1]
    nk = s_pad // kb
    n_sel = min(TOPK_KEYS, s_true // 4)
    kit = ki_all.reshape(bsz, nk, kb, IDX_DIM).transpose(0, 1, 3, 2)
    kt = k_all.reshape(bsz, nk, kb, N_KV_HEADS * HEAD_DIM).transpose(0, 1, 3, 2)
    vv = v_all.reshape(bsz, nk, kb, N_KV_HEADS * HEAD_DIM)
    qrow = lambda b, i: (b, i, 0)
    kmap = lambda b, i: (b, 0, 0, 0)
    kern = functools.partial(_dsa_kernel, qb=qb, kb=kb, s_true=s_true, pos0=pos0, n_sel=n_sel)
    return pl.pallas_call(
        kern,
        grid=(bsz, t // qb),
        in_specs=[pl.BlockSpec((None, qb, q.shape[2]), qrow),
                  pl.BlockSpec((None, qb, qi.shape[2]), qrow),
                  pl.BlockSpec((None, qb, wi.shape[2]), qrow),
                  _resident((None, nk, IDX_DIM, kb), kmap),
                  _resident((None, nk, N_KV_HEADS * HEAD_DIM, kb), kmap),
                  _resident((None, nk, kb, N_KV_HEADS * HEAD_DIM), kmap)],
        out_specs=pl.BlockSpec((None, qb, q.shape[2]), qrow),
        out_shape=jax.ShapeDtypeStruct(q.shape, f32),
        scratch_shapes=[pltpu.VMEM((nk, qb, kb), i32),
                        pltpu.VMEM((N_KV_HEADS, HEAD_DIM + qb, kb), MXU_DTYPE),
                        pltpu.VMEM((N_KV_HEADS, GROUP * qb, kb), f32),
                        pltpu.VMEM((N_KV_HEADS, GROUP * qb, kb), f32),
                        pltpu.VMEM((N_KV_HEADS, GROUP * qb, 1), f32),
                        pltpu.VMEM((N_KV_HEADS, GROUP * qb, 1), f32),
                        pltpu.VMEM((N_KV_HEADS, kb, 2 * HEAD_DIM), MXU_DTYPE),
                        pltpu.VMEM((N_KV_HEADS, GROUP * qb, 1), f32),
                        pltpu.VMEM((N_KV_HEADS, GROUP * qb, 2 * HEAD_DIM), f32),
                        pltpu.VMEM((kb, kb), MXU_DTYPE)],
        compiler_params=_cparams(("parallel", "arbitrary")),
        name="dsa_attention",
    )(q, qi, wi, kit, kt, vv)


def _expm1(y):
    u = jnp.exp(y)
    um1 = u - 1.0
    return jnp.where(um1 == 0.0, y, jnp.where(um1 == -1.0, -1.0, um1 * y / jnp.log(u)))


def _rglru_kernel(xr_ref, gr_ref, cs_ref, h0_ref, wc_ref, bc_ref, wax_ref, ba_ref, bx_ref, lam_ref,
                  rnn_ref, hlast_ref, xp_scr, a_scr, b_scr, h_scr, *, reset_first):
    t = pl.program_id(1)
    tt, c = xr_ref.shape
    lead = SUBLANES

    @pl.when(t == 0)
    def _():
        xp_scr[lead - (CONV_W - 1):lead, :] = cs_ref[...]
        h_scr[...] = h0_ref[...]

    xp_scr[lead:lead + tt, :] = xr_ref[...]
    conv = bc_ref[...]
    for j in range(CONV_W):
        off = lead - (CONV_W - 1) + j
        conv = conv + xp_scr[off:off + tt, :] * wc_ref[j:j + 1, :]
    xp_scr[lead - (CONV_W - 1):lead, :] = xp_scr[lead + tt - (CONV_W - 1):lead + tt, :]

    cb = conv.astype(MXU_DTYPE)
    sp = lam_ref[...]
    grow = t * tt + lax.broadcasted_iota(i32, (tt, 1), 0)
    for g in range(c // RNN_GROUP):
        sl = slice(g * RNN_GROUP, (g + 1) * RNN_GROUP)
        z = _dot(cb[:, sl], wax_ref[g])
        r = jax.nn.sigmoid(z[:, :RNN_GROUP] + ba_ref[:, sl])
        ig = jax.nn.sigmoid(z[:, RNN_GROUP:] + bx_ref[:, sl])
        log_a = -LRU_C * r * sp[:, sl]
        mult = jnp.sqrt(-_expm1(2.0 * log_a))
        if reset_first:
            mult = jnp.where(grow == 0, 1.0, mult)
        a_scr[:, sl] = jnp.exp(log_a)
        b_scr[:, sl] = mult * (ig * conv[:, sl])

    def rows(jb, h):
        base = pl.multiple_of(jb * SUBLANES, SUBLANES)
        for u in range(SUBLANES):
            h = a_scr[pl.ds(base + u, 1), :] * h + b_scr[pl.ds(base + u, 1), :]
            rnn_ref[pl.ds(base + u, 1), :] = h
        return h

    h = lax.fori_loop(0, tt // SUBLANES, rows, h_scr[...])
    h_scr[...] = h
    rnn_ref[...] = rnn_ref[...] * jax.nn.gelu(gr_ref[...])

    @pl.when(t == pl.num_programs(1) - 1)
    def _():
        hlast_ref[...] = h


def _blockdiag_groups(w):
    per = RNN_GROUP // RNN_BLOCK
    g = w.reshape(N_RNN_BLOCKS // per, per, RNN_BLOCK, RNN_BLOCK)
    eye = jnp.eye(per, dtype=w.dtype)
    return jnp.einsum('gacd,ab->gacbd', g, eye).reshape(N_RNN_BLOCKS // per, RNN_GROUP, RNN_GROUP)


def _rglru(xr, gr, conv_state, h0, w_conv, b_conv, w_a, b_a, w_x, b_x, lam, *, tt, reset_first):
    bsz, t, c = xr.shape
    wax = jnp.concatenate([_blockdiag_groups(w_a), _blockdiag_groups(w_x)], axis=2).astype(MXU_DTYPE)
    sp = jax.nn.softplus(-lam.astype(f32)).reshape(1, c)
    seq = lambda b, i: (b, i, 0)
    per_b = lambda b, i: (b, 0, 0)
    const2 = lambda b, i: (0, 0)
    const3 = lambda b, i: (0, 0, 0)
    kern = functools.partial(_rglru_kernel, reset_first=reset_first)
    return pl.pallas_call(
        kern,
        grid=(bsz, t // tt),
        in_specs=[pl.BlockSpec((None, tt, c), seq), pl.BlockSpec((None, tt, c), seq),
                  pl.BlockSpec((None, CONV_W - 1, c), per_b), pl.BlockSpec((None, 1, c), per_b),
                  pl.BlockSpec((CONV_W, c), const2), pl.BlockSpec((1, c), const2),
                  pl.BlockSpec(wax.shape, const3), pl.BlockSpec((1, c), const2),
                  pl.BlockSpec((1, c), const2), pl.BlockSpec((1, c), const2)],
        out_specs=[pl.BlockSpec((None, tt, c), seq), pl.BlockSpec((None, 1, c), per_b)],
        out_shape=[jax.ShapeDtypeStruct((bsz, t, c), f32), jax.ShapeDtypeStruct((bsz, 1, c), f32)],
        scratch_shapes=[pltpu.VMEM((tt + SUBLANES, c), f32), pltpu.VMEM((tt, c), f32),
                        pltpu.VMEM((tt, c), f32), pltpu.VMEM((1, c), f32)],
        compiler_params=_cparams(("parallel", "arbitrary")),
        name="rglru",
    )(xr, gr, conv_state, h0.reshape(bsz, 1, c), w_conv, b_conv.reshape(1, c), wax,
      b_a.reshape(1, c), b_x.reshape(1, c), sp)


def _layer_norm(z, g, b):
    mu = jnp.mean(z, axis=-1, keepdims=True)
    var = jnp.mean(jnp.square(z - mu), axis=-1, keepdims=True)
    return (z - mu) * lax.rsqrt(var + LN_EPS) * g + b


def _merge_kernel(attn_ref, rnn_ref, ga_ref, gb_ref, x_ref, wo_ref, g1_ref, b1_ref, wr_ref, br_ref,
                  x1_ref, x1b_ref, cw_ref, *, dn_alpha):
    merged = jax.nn.sigmoid(ga_ref[...]) * attn_ref[...] + jax.nn.sigmoid(gb_ref[...]) * rnn_ref[...]
    m = _dot(merged.astype(MXU_DTYPE), wo_ref[...])
    x1 = _layer_norm(dn_alpha * x_ref[...] + m, g1_ref[...], b1_ref[...])
    x1_ref[...] = x1
    x1b = x1.astype(MXU_DTYPE)
    x1b_ref[...] = x1b
    logits = _dot(x1b, wr_ref[...]) + br_ref[...]
    ne = logits.shape[1]
    eidx = lax.broadcasted_iota(i32, logits.shape, 1).astype(f32)
    work = logits
    chosen = jnp.zeros(logits.shape, jnp.bool_)
    top = None
    for kk in range(TOP_K):
        mx = jnp.max(work, axis=1, keepdims=True)
        if kk == 0:
            top = mx
        first = jnp.min(jnp.where(work == mx, eidx, ne), axis=1, keepdims=True)
        pick = eidx == first
        chosen = chosen | pick
        work = jnp.where(pick, NEG_INF, work)
    e = jnp.where(chosen, jnp.exp(logits - top), 0.0)
    cw_ref[...] = e / jnp.sum(e, axis=1, keepdims=True)


def _merge_ln_router(attn, rnn, ga, gb, x2d, w_out, ln_g, ln_b, w_router, b_router, *, tm, dn_alpha):
    n, d = x2d.shape
    ne = w_router.shape[1]
    row = lambda i: (i, 0)
    const = lambda i: (0, 0)
    kern = functools.partial(_merge_kernel, dn_alpha=dn_alpha)
    return pl.pallas_call(
        kern,
        grid=(n // tm,),
        in_specs=[pl.BlockSpec((tm, d), row)] * 5
                 + [pl.BlockSpec((d, d), const), pl.BlockSpec((1, d), const), pl.BlockSpec((1, d), const),
                    pl.BlockSpec((d, ne), const), pl.BlockSpec((1, ne), const)],
        out_specs=[pl.BlockSpec((tm, d), row), pl.BlockSpec((tm, d), row), pl.BlockSpec((tm, ne), row)],
        out_shape=[jax.ShapeDtypeStruct((n, d), f32), jax.ShapeDtypeStruct((n, d), MXU_DTYPE),
                   jax.ShapeDtypeStruct((n, ne), f32)],
        compiler_params=_cparams(("parallel",)),
        name="merge_ln_router",
    )(attn, rnn, ga, gb, x2d, w_out.astype(MXU_DTYPE), ln_g.reshape(1, d), ln_b.reshape(1, d),
      w_router.astype(MXU_DTYPE), b_router.reshape(1, ne))


MOE_CHUNK = 128
MOE_SUBTILE = 1024


def _moe_kernel(xb_ref, cwt_ref, wg_ref, bg_ref, wu_ref, bu_ref, wd_ref, bd_ref, o_ref, rankt_scr):
    e = pl.program_id(1)
    tm = MOE_SUBTILE if cwt_ref.shape[1] % MOE_SUBTILE == 0 else cwt_ref.shape[1]

    @pl.when(e == 0)
    def _():
        upper = jnp.where(lax.broadcasted_iota(i32, (tm, tm), 0) < lax.broadcasted_iota(i32, (tm, tm), 1),
                          1.0, 0.0).astype(MXU_DTYPE)
        for s0 in range(0, cwt_ref.shape[1], tm):
            selt = jnp.where(cwt_ref[:, s0:s0 + tm] > 0.0, 1.0, 0.0)
            rankt = _dot(selt.astype(MXU_DTYPE), upper)
            rankt_scr[:, s0:s0 + tm] = jnp.where(selt > 0.0, rankt, -1.0)
        o_ref[...] = jnp.zeros(o_ref.shape, f32)

    for s0 in range(0, cwt_ref.shape[1], tm):
        _moe_subtile(e, s0, tm, xb_ref, cwt_ref, wg_ref, bg_ref, wu_ref, bu_ref, wd_ref, bd_ref, o_ref, rankt_scr)


def _moe_subtile(e, s0, tm, xb_ref, cwt_ref, wg_ref, bg_ref, wu_ref, bu_ref, wd_ref, bd_ref, o_ref, rankt_scr):
    rrow = rankt_scr[pl.ds(e, 1), s0:s0 + tm]
    grow = cwt_ref[pl.ds(e, 1), s0:s0 + tm]
    cnt = jnp.max(rrow).astype(i32) + 1

    def run_chunk(first_rank, size):
        lo = first_rank.astype(f32)
        hit = rrow - lo == lax.broadcasted_iota(i32, (size, tm), 0).astype(f32)
        gather = jnp.where(hit, 1.0, 0.0).astype(MXU_DTYPE)
        gcomp = jnp.sum(jnp.where(hit, grow, 0.0), axis=1, keepdims=True)
        xg = _dot(gather, xb_ref[s0:s0 + tm, :]).astype(MXU_DTYPE)
        g = _dot(xg, wg_ref[...]) + bg_ref[...]
        u = _dot(xg, wu_ref[...]) + bu_ref[...]
        g = jnp.minimum(g, SWIGLU_LIMIT)
        u = jnp.clip(u, -SWIGLU_LIMIT, SWIGLU_LIMIT)
        hmid = g * jax.nn.sigmoid(SWIGLU_ALPHA * g) * (u + 1.0)
        y = (_dot(hmid.astype(MXU_DTYPE), wd_ref[...]) + bd_ref[...]) * gcomp
        y_hi = y.astype(MXU_DTYPE)
        y_lo = (y - y_hi.astype(f32)).astype(MXU_DTYPE)
        o_ref[s0:s0 + tm, :] += lax.dot_general(jnp.concatenate([gather, gather], axis=0),
                                                jnp.concatenate([y_hi, y_lo], axis=0),
                                                (((0,), (0,)), ((), ())), preferred_element_type=f32)

    def chunk(cidx, carry):
        run_chunk(cidx * MOE_CHUNK, MOE_CHUNK)
        return carry

    lax.fori_loop(0, (cnt + MOE_CHUNK - 1) // MOE_CHUNK, chunk, 0)


def _moe(x1b, cw, w_gate, b_gate, w_up, b_up, w_down, b_down, *, tm):
    n, d = x1b.shape
    ne = cw.shape[1]
    dff = w_gate.shape[2]
    row = lambda i, e: (i, 0)
    wmap = lambda i, e: (e, 0, 0)
    return pl.pallas_call(
        _moe_kernel,
        grid=(n // tm, ne),
        in_specs=[pl.BlockSpec((tm, d), row), pl.BlockSpec((ne, tm), lambda i, e: (0, i)),
                  pl.BlockSpec((None, d, dff), wmap), pl.BlockSpec((None, 1, dff), wmap),
                  pl.BlockSpec((None, d, dff), wmap), pl.BlockSpec((None, 1, dff), wmap),
                  pl.BlockSpec((None, dff, d), wmap), pl.BlockSpec((None, 1, d), wmap)],
        out_specs=pl.BlockSpec((tm, d), row),
        out_shape=jax.ShapeDtypeStruct((n, d), f32),
        scratch_shapes=[pltpu.VMEM((ne, tm), f32)],
        compiler_params=_cparams(("parallel", "arbitrary")),
        name="moe_experts",
    )(x1b, cw.T, w_gate.astype(MXU_DTYPE), b_gate.reshape(ne, 1, dff), w_up.astype(MXU_DTYPE),
      b_up.reshape(ne, 1, dff), w_down.astype(MXU_DTYPE), b_down.reshape(ne, 1, d))


def _final_kernel(x1_ref, ffn_ref, p_ref, g2_ref, b2_ref, wpg_ref, wpp_ref, y_ref, *, dn_alpha):
    x2 = _layer_norm(dn_alpha * x1_ref[...] + ffn_ref[...], g2_ref[...], b2_ref[...])
    gate = jax.nn.sigmoid(_dot(x2.astype(MXU_DTYPE), wpg_ref[...]))
    emb = _dot(p_ref[...].astype(MXU_DTYPE), wpp_ref[...])
    y_ref[...] = x2 + gate * emb


def _final(x1, ffn, p2d, ln_g, ln_b, w_ple_gate, w_ple_proj, *, tm, dn_alpha):
    n, d = x1.shape
    dp = p2d.shape[1]
    row = lambda i: (i, 0)
    const = lambda i: (0, 0)
    kern = functools.partial(_final_kernel, dn_alpha=dn_alpha)
    return pl.pallas_call(
        kern,
        grid=(n // tm,),
        in_specs=[pl.BlockSpec((tm, d), row), pl.BlockSpec((tm, d), row), pl.BlockSpec((tm, dp), row),
                  pl.BlockSpec((1, d), const), pl.BlockSpec((1, d), const),
                  pl.BlockSpec((d, d), const), pl.BlockSpec((dp, d), const)],
        out_specs=pl.BlockSpec((tm, d), row),
        out_shape=jax.ShapeDtypeStruct((n, d), f32),
        compiler_params=_cparams(("parallel",)),
        name="ln2_ple",
    )(x1, ffn, p2d, ln_g.reshape(1, d), ln_b.reshape(1, d), w_ple_gate.astype(MXU_DTYPE),
      w_ple_proj.astype(MXU_DTYPE))


ROW_TILE = 512
TIME_TILE = 512
MOE_TILE = 2048
KEY_BLOCK = 1024
KEY_BLOCK_SMALL = 3 * LANES


def _pick_tile(n, pref):
    t = min(pref, n)
    while n % t:
        t //= 2
    return t


def _layer(x, p, pos0, past_k, past_v, past_ki, conv_state, h0, prm, depth):
    (w_in, w_conv, b_conv, w_a, b_a, w_x, b_x, lam, w_out, ln1_g, ln1_b, w_router, b_router,
     w_gate, b_gate, w_up, b_up, w_down, b_down, ln2_g, ln2_b, w_ple_gate, w_ple_proj) = prm
    bsz, t, d = x.shape
    n = bsz * t
    past = past_k.shape[1]
    dn_alpha = (2 * depth) ** 0.25

    tm = _pick_tile(n, ROW_TILE)
    pos = pos0 + jnp.arange(t)
    tabs_a = _rope_tables(pos, ROPE_DIM, HEAD_DIM)
    tabs_i = _rope_tables(pos, IDX_ROPE_DIM, IDX_DIM)
    if t % tm:
        rep = tm // t
        tabs_a = tuple(jnp.tile(a, (rep, 1)) for a in tabs_a)
        tabs_i = tuple(jnp.tile(a, (rep, 1)) for a in tabs_i)
    (q, k, kb, v, vb, qi, ki, kib, wi, xr, gr, ga, gb) = _in_projection(
        x.reshape(n, d), _pack_w_in(w_in), tabs_a, tabs_i, tm)

    s_true = past + t
    qb = min(Q_BLOCK, t)
    kblk = KEY_BLOCK if s_true % KEY_BLOCK == 0 else KEY_BLOCK_SMALL
    s_pad = -(-s_true // kblk) * kblk

    def keys_by_position(past_arr, new_arr):
        new_arr = new_arr.reshape(bsz, t, -1)
        parts = [new_arr]
        if past:
            parts.insert(0, past_arr.reshape(bsz, past, new_arr.shape[2]).astype(MXU_DTYPE))
        if s_pad > s_true:
            parts.append(jnp.zeros((bsz, s_pad - s_true, new_arr.shape[2]), MXU_DTYPE))
        return jnp.concatenate(parts, axis=1)

    attn = _dsa_attention(q.reshape(bsz, t, -1), qi.reshape(bsz, t, -1), wi.reshape(bsz, t, -1),
                          keys_by_position(past_ki, kib), keys_by_position(past_k, kb),
                          keys_by_position(past_v, vb), qb=qb, kb=kblk, s_true=s_true, pos0=pos0)

    tt = _pick_tile(t, TIME_TILE)
    xr3 = xr.reshape(bsz, t, -1)
    rnn, h_last = _rglru(xr3, gr.reshape(bsz, t, -1), conv_state, h0, w_conv, b_conv, w_a, b_a, w_x, b_x,
                         lam, tt=tt, reset_first=(pos0 == 0))
    new_conv = jnp.concatenate([conv_state, xr3], axis=1)[:, -(CONV_W - 1):]

    x1, x1b, cw = _merge_ln_router(attn.reshape(n, d), rnn.reshape(n, d), ga, gb, x.reshape(n, d), w_out,
                                   ln1_g, ln1_b, w_router, b_router, tm=tm, dn_alpha=dn_alpha)
    ffn = _moe(x1b, cw, w_gate, b_gate, w_up, b_up, w_down, b_down, tm=_pick_tile(n, MOE_TILE))
    y = _final(x1, ffn, p.reshape(n, -1), ln2_g, ln2_b, w_ple_gate, w_ple_proj, tm=tm, dn_alpha=dn_alpha)

    return (y.reshape(bsz, t, d), k.reshape(bsz, t, N_KV_HEADS, HEAD_DIM), v.reshape(bsz, t, N_KV_HEADS, HEAD_DIM),
            ki.reshape(bsz, t, IDX_DIM), new_conv, h_last.reshape(bsz, -1))


def kernel(x_prompt, x_sample, cache_k, cache_v, cache_kidx, state_conv, state_h, p_prompt, p_sample,
           w_in, w_conv, b_conv, w_a, b_a, w_x, b_x, lru_lambda, w_out, ln1_g, ln1_b, w_router, b_router,
           w_gate, b_gate, w_up, b_up, w_down, b_down, ln2_g, ln2_b, w_ple_gate, w_ple_proj):
    depth = w_in.shape[0]
    bp = x_prompt.shape[0]
    past_len = cache_k.shape[2]
    dt = x_prompt.dtype
    empty_kv = jnp.zeros((bp, 0, N_KV_HEADS, HEAD_DIM), dt)
    empty_ki = jnp.zeros((bp, 0, IDX_DIM), dt)
    zero_conv = jnp.zeros((bp, CONV_W - 1, x_prompt.shape[2]), dt)
    zero_h = jnp.zeros((bp, x_prompt.shape[2]), dt)

    yp, ys = x_prompt, x_sample
    outs_p, outs_s = [], []
    for l in range(depth):
        prm = (w_in[l], w_conv[l], b_conv[l], w_a[l], b_a[l], w_x[l], b_x[l], lru_lambda[l], w_out[l],
               ln1_g[l], ln1_b[l], w_router[l], b_router[l], w_gate[l], b_gate[l], w_up[l], b_up[l],
               w_down[l], b_down[l], ln2_g[l], ln2_b[l], w_ple_gate[l], w_ple_proj[l])
        yp, *rest_p = _layer(yp, p_prompt[l], 0, empty_kv, empty_kv, empty_ki, zero_conv, zero_h, prm, depth)
        ys, *rest_s = _layer(ys, p_sample[l], past_len, cache_k[l], cache_v[l], cache_kidx[l],
                             state_conv[l], state_h[l], prm, depth)
        outs_p.append(rest_p)
        outs_s.append(rest_s)

    stack = lambda outs, j: jnp.stack([o[j] for o in outs])
    return (yp, ys) + tuple(stack(outs_p, j) for j in range(5)) + tuple(stack(outs_s, j) for j in range(5))
```

```python
import functools

import jax
import jax.numpy as jnp
import numpy as np
from jax import lax
from jax.experimental import pallas as pl
from jax.experimental.pallas import tpu as pltpu

f32 = jnp.float32
i32 = jnp.int32
MXU_DTYPE = jnp.bfloat16

CHUNK = 64
HEAD_DIM = 128
N_KV_HEADS = 2
GROUP = 4
ROPE_DIM = 32
ROPE_THETA = 500000.0
N_IDX_HEADS = 8
IDX_DIM = 64
IDX_ROPE_DIM = 16
IDX_SCALE = (IDX_DIM * N_IDX_HEADS) ** -0.5
TOPK_KEYS = 256
Q_BLOCK = 128
N_RNN_BLOCKS = 16
RNN_BLOCK = 64
RNN_GROUP = 256
CONV_W = 4
LRU_C = 8.0
N_EXPERTS = 32
TOP_K = 4
SWIGLU_LIMIT = 7.0
SWIGLU_ALPHA = 1.702
LN_EPS = 1e-5

LANES = 128
SUBLANES = 8
VMEM_LIMIT = 56 * 1024 * 1024
INT_MIN = -2 ** 31
NEG_INF = float("-inf")
LOG2_E = 1.4426950408889634


def _cparams(sem):
    return pltpu.CompilerParams(dimension_semantics=sem, vmem_limit_bytes=VMEM_LIMIT)


def _resident(shape, index_map):
    return pl.BlockSpec(shape, index_map, pipeline_mode=pl.Buffered(1))


def _dot(a, b):
    return jnp.dot(a, b, preferred_element_type=f32)


D_MODEL = N_KV_HEADS * GROUP * HEAD_DIM
D_KV = N_KV_HEADS * HEAD_DIM
D_QI = N_IDX_HEADS * IDX_DIM
_SEG_WIDTHS = (D_MODEL, D_KV, D_KV, D_QI, LANES, D_MODEL, D_MODEL, D_MODEL, D_MODEL)
_C_Q, _C_K, _C_V, _C_QI, _C_KW, _C_XR, _C_GR, _C_GA, _C_GB, _C_END = (
    int(c) for c in np.concatenate([[0], np.cumsum(_SEG_WIDTHS)]))


def _rope_tile(h, c, s_lo, s_hi, half):
    return h * c + pltpu.roll(h, half, 1) * s_hi + pltpu.roll(h, LANES - half, 1) * s_lo


def _inproj_kernel(x_ref, w_ref, ca_ref, sla_ref, sha_ref, ci_ref, sli_ref, shi_ref,
                   q_ref, k_ref, kb_ref, v_ref, vb_ref, qi_ref, ki_ref, kib_ref, wi_ref,
                   xr_ref, gr_ref, ga_ref, gb_ref):
    xb = x_ref[...].astype(MXU_DTYPE)

    def proj(a, b):
        return _dot(xb, w_ref[:, a:b])

    ca, sla, sha = ca_ref[...], sla_ref[...], sha_ref[...]
    ci, sli, shi = ci_ref[...], sli_ref[...], shi_ref[...]

    hq = proj(_C_Q, _C_K)
    for j in range(hq.shape[1] // LANES):
        t = _rope_tile(hq[:, j * LANES:(j + 1) * LANES], ca, sla, sha, ROPE_DIM // 2)
        q_ref[:, j * LANES:(j + 1) * LANES] = t.astype(q_ref.dtype)
    hk = proj(_C_K, _C_V)
    for j in range(hk.shape[1] // LANES):
        t = _rope_tile(hk[:, j * LANES:(j + 1) * LANES], ca, sla, sha, ROPE_DIM // 2)
        k_ref[:, j * LANES:(j + 1) * LANES] = t
        kb_ref[:, j * LANES:(j + 1) * LANES] = t.astype(kb_ref.dtype)
    hv = proj(_C_V, _C_QI)
    v_ref[...] = hv
    vb_ref[...] = hv.astype(vb_ref.dtype)
    hqi = proj(_C_QI, _C_KW)
    for j in range(hqi.shape[1] // LANES):
        t = _rope_tile(hqi[:, j * LANES:(j + 1) * LANES], ci, sli, shi, IDX_ROPE_DIM // 2)
        qi_ref[:, j * LANES:(j + 1) * LANES] = t.astype(qi_ref.dtype)
    hkw = proj(_C_KW, _C_XR)
    t = _rope_tile(hkw, ci, sli, shi, IDX_ROPE_DIM // 2)
    ki_ref[...] = t[:, :IDX_DIM]
    kib_ref[...] = t[:, :IDX_DIM].astype(kib_ref.dtype)
    wi_ref[...] = hkw[:, IDX_DIM:IDX_DIM + N_IDX_HEADS]
    xr_ref[...] = proj(_C_XR, _C_GR)
    gr_ref[...] = proj(_C_GR, _C_GA)
    ga_ref[...] = proj(_C_GA, _C_GB)
    gb_ref[...] = proj(_C_GB, _C_END)


def _rope_tables(pos, rot_dim, head_dim):
    half = rot_dim // 2
    inv = ROPE_THETA ** (-jnp.arange(half, dtype=f32) / half)
    ang = pos.astype(f32)[:, None] * inv[None, :]
    cos, sin = jnp.cos(ang), jnp.sin(ang)
    n = pos.shape[0]
    one = jnp.ones((n, head_dim - rot_dim), f32)
    zero_h = jnp.zeros((n, half), f32)
    zero_r = jnp.zeros((n, head_dim - rot_dim), f32)
    c = jnp.concatenate([cos, cos, one], axis=1)
    s_lo = jnp.concatenate([-sin, zero_h, zero_r], axis=1)
    s_hi = jnp.concatenate([zero_h, sin, zero_r], axis=1)
    rep = LANES // head_dim
    return tuple(jnp.tile(t, (1, rep)) for t in (c, s_lo, s_hi))


def _pack_w_in(w_in):
    d = w_in.shape[0]
    pad = jnp.zeros((d, LANES - IDX_DIM - N_IDX_HEADS), w_in.dtype)
    split = _C_KW + IDX_DIM + N_IDX_HEADS
    return jnp.concatenate([w_in[:, :split], pad, w_in[:, split:]], axis=1).astype(MXU_DTYPE)


def _in_projection(x2d, w_packed, tabs_a, tabs_i, tm):
    n, d = x2d.shape
    nt = tabs_a[0].shape[0] // tm
    row = lambda i: (i, 0)
    tab = lambda i: (i % nt, 0)
    out_cols = [(D_MODEL, MXU_DTYPE), (D_KV, f32), (D_KV, MXU_DTYPE), (D_KV, f32), (D_KV, MXU_DTYPE),
                (D_QI, MXU_DTYPE), (IDX_DIM, f32), (IDX_DIM, MXU_DTYPE), (N_IDX_HEADS, f32),
                (D_MODEL, f32), (D_MODEL, f32), (D_MODEL, f32), (D_MODEL, f32)]
    return pl.pallas_call(
        _inproj_kernel,
        grid=(n // tm,),
        in_specs=[pl.BlockSpec((tm, d), row), _resident(w_packed.shape, lambda i: (0, 0))]
                 + [pl.BlockSpec((tm, LANES), tab)] * 6,
        out_specs=[pl.BlockSpec((tm, c), row) for c, _ in out_cols],
        out_shape=[jax.ShapeDtypeStruct((n, c), dt) for c, dt in out_cols],
        compiler_params=_cparams(("parallel",)),
        name="in_projection",
    )(x2d, w_packed, *tabs_a, *tabs_i)


MASKED_LOGIT = -2.0 ** 126
MAX_FLOOR = -2.0 ** 120
UNKNOWN_COUNT = 1e9
MIN_NORMAL = 2.0 ** -126


SEARCH_FALLBACK = 40


def _score_key(s):
    bits = pltpu.bitcast(s, i32)
    return bits ^ ((bits >> 31) & 0x7FFFFFFF)


def _key_score(k):
    return pltpu.bitcast(k ^ ((k >> 31) & 0x7FFFFFFF), f32)


def _dsa_kernel(q_ref, qi_ref, wi_ref, kit_ref, kt_ref, v_ref, o_ref,
                keys_scr, kaug_scr, sa_scr, sb_scr, ma_scr, mb_scr, vaug_scr, m_scr, acc_scr, upper_scr,
                *, qb, kb, s_true, pos0, n_sel):
    i = pl.program_id(1)

    @pl.when(i == 0)
    def _():
        upper_scr[...] = jnp.where(lax.broadcasted_iota(i32, (kb, kb), 0) < lax.broadcasted_iota(i32, (kb, kb), 1),
                                   1.0, 0.0).astype(MXU_DTYPE)
        ones_col = jnp.where(lax.broadcasted_iota(i32, (kb, HEAD_DIM), 1) == 0, 1.0, 0.0).astype(MXU_DTYPE)
        for g in range(N_KV_HEADS):
            vaug_scr[g, :, HEAD_DIM:] = ones_col

    row = lax.broadcasted_iota(i32, (qb, 1), 0)
    pos = pos0 + i * qb + row
    vis_end = jnp.minimum((pos // CHUNK + 1) * CHUNK, s_true)
    pos_last = pos0 + i * qb + (qb - 1)
    kend = jnp.minimum((pos_last // CHUNK + 1) * CHUNK, s_true)
    nkb = (kend + kb - 1) // kb
    lane = lax.broadcasted_iota(i32, (qb, kb), 1)

    qi = qi_ref[...]
    wi = wi_ref[...]
    qi_rows = jnp.concatenate([qi[:, h * IDX_DIM:(h + 1) * IDX_DIM] for h in range(N_IDX_HEADS)], axis=0)

    def score_block(b, carry):
        kit = kit_ref[b]
        sc = jnp.zeros((qb, kb), f32)
        d_all = _dot(qi_rows, kit)
        for h in range(N_IDX_HEADS):
            sc = sc + wi[:, h:h + 1] * jnp.maximum(d_all[h * qb:(h + 1) * qb, :], 0.0)
        sc = sc * IDX_SCALE
        sc = jnp.where(jnp.abs(sc) < MIN_NORMAL, 0.0, sc)
        adm = b * kb + lane < vis_end
        keys_scr[b] = jnp.where(adm, _score_key(sc), INT_MIN)
        smax, smin = carry
        hi_part = jnp.where(adm, sc, NEG_INF)
        lo_part = jnp.where(adm, sc, -NEG_INF)
        for t in range(kb // LANES):
            smax = jnp.maximum(smax, hi_part[:, t * LANES:(t + 1) * LANES])
            smin = jnp.minimum(smin, lo_part[:, t * LANES:(t + 1) * LANES])
        return smax, smin

    smax, smin = lax.fori_loop(0, nkb, score_block,
                               (jnp.full((qb, LANES), NEG_INF, f32), jnp.full((qb, LANES), -NEG_INF, f32)))
    smax = jnp.max(smax, axis=1, keepdims=True)
    smin = jnp.min(smin, axis=1, keepdims=True)

    def count(pred):
        def body(b, acc):
            for t in range(kb // LANES):
                hit = pred(keys_scr[b, :, t * LANES:(t + 1) * LANES], b * kb + t * LANES)
                acc = acc + jnp.where(hit, 1.0, 0.0)
            return acc
        acc = lax.fori_loop(0, nkb, body, jnp.zeros((qb, LANES), f32))
        return jnp.sum(acc, axis=1, keepdims=True)

    def wide(x):
        return jnp.broadcast_to(x, (qb, LANES))

    nsel = jnp.float32(n_sel)

    def unsettled(ct):
        return jnp.max(jnp.abs(ct - nsel)) > 0.0

    def search_cond(carry):
        t, lo, hi, clo = carry[:4]
        open_rows = jnp.where((clo == nsel) | (hi == lo + 1), 0.0, 1.0)
        return jnp.max(open_rows) > 0.0

    def search_one(t, lo, hi, clo, glo, ghi, side):
        vlo, vhi = _key_score(lo), _key_score(hi - 1)
        frac = glo / (glo - ghi)
        cand = _score_key(vlo + (vhi - vlo) * frac)
        cand = jnp.where(t >= SEARCH_FALLBACK, lo + jnp.right_shift(hi - lo, 1), cand)
        cand = jnp.where((t == 0) & (lo < 0) & (hi > 0), 0, cand)
        cand = jnp.where((t == 1) & (lo <= 0) & (hi > 1), 1, cand)
        cand = jnp.maximum(lo + 1, jnp.minimum(cand, hi - 1))
        cw = wide(cand)
        c = count(lambda k, c0: k >= cw)
        ok = c >= nsel
        g = jnp.log(jnp.maximum(c, 0.5) / nsel)
        ghi = jnp.where(ok & (side > 0), 0.5 * ghi, ghi)
        glo = jnp.where(jnp.logical_not(ok) & (side < 0), 0.5 * glo, glo)
        return (jnp.where(ok, cand, lo), jnp.where(ok, hi, cand), jnp.where(ok, c, clo),
                jnp.where(ok, g, glo), jnp.where(ok, ghi, g), jnp.where(ok, 1, -1))

    def search_step(carry):
        t, state = carry[0], carry[1:]
        state = search_one(t, *state)
        state = search_one(t + 1, *state)
        return (t + 2,) + state

    nvis = vis_end.astype(f32)
    short = nvis <= nsel
    lo0 = jnp.where(short, INT_MIN, _score_key(smin))
    hi0 = jnp.where(short, INT_MIN + 1, _score_key(smax) + 1)
    clo0 = jnp.where(short, UNKNOWN_COUNT, nvis)
    glo0 = jnp.log(jnp.maximum(nvis, nsel) / nsel)
    ghi0 = jnp.full((qb, 1), np.log(0.5 / n_sel), f32)
    _, tau, _, ctau, _, _, _ = lax.while_loop(
        search_cond, search_step, (jnp.int32(0), lo0, hi0, clo0, glo0, ghi0, jnp.zeros((qb, 1), i32)))

    @pl.when(unsettled(ctau))
    def _():
        tw = wide(tau)
        need = nsel - count(lambda k, c0: k > tw)
        need = jnp.where(tau == INT_MIN, UNKNOWN_COUNT, need)

        def demote(b, seen):
            k = keys_scr[b]
            eq = k == tau
            before = _dot(jnp.where(eq, 1.0, 0.0).astype(MXU_DTYPE), upper_scr[...]) + seen
            keys_scr[b] = jnp.where(eq & (before >= need), tau - 1, k)
            return seen + jnp.sum(jnp.where(eq, 1.0, 0.0), axis=1, keepdims=True)

        lax.fori_loop(0, nkb, demote, jnp.zeros((qb, 1), f32))

    tau_sel = jnp.maximum(tau, INT_MIN + 1)

    q = q_ref[...]
    rq = lax.broadcasted_iota(i32, (GROUP * qb, qb), 0) % qb
    onehot = jnp.where(rq == lax.broadcasted_iota(i32, (GROUP * qb, qb), 1), 1.0, 0.0).astype(MXU_DTYPE)
    qaug = [jnp.concatenate([jnp.concatenate([q[:, (g * GROUP + j) * HEAD_DIM:(g * GROUP + j + 1) * HEAD_DIM]
                                              for j in range(GROUP)], axis=0), onehot], axis=1)
            for g in range(N_KV_HEADS)]
    m_scr[...] = jnp.full(m_scr.shape, MAX_FLOOR, f32)
    acc_scr[...] = jnp.zeros(acc_scr.shape, f32)
    c2 = HEAD_DIM ** -0.5 * LOG2_E

    def logits(b, s_ref, smax_ref):
        k = keys_scr[b]
        mask_rows = jnp.where(k >= tau_sel, 0.0, MASKED_LOGIT).astype(MXU_DTYPE)
        for g in range(N_KV_HEADS):
            kaug_scr[g, :HEAD_DIM, :] = kt_ref[b, g * HEAD_DIM:(g + 1) * HEAD_DIM, :]
            kaug_scr[g, HEAD_DIM:, :] = mask_rows
        for g in range(N_KV_HEADS):
            s = _dot(qaug[g], kaug_scr[g]) * c2
            s_ref[g] = s
            smax_ref[g] = jnp.max(s, axis=1, keepdims=True)

    def update(b, s_ref, smax_ref):
        for g in range(N_KV_HEADS):
            m_old = m_scr[g]
            m_new = jnp.maximum(m_old, smax_ref[g])
            p = jnp.exp2(s_ref[g] - m_new)
            alpha = jnp.exp2(m_old - m_new)
            vaug_scr[g, :, :HEAD_DIM] = v_ref[b, :, g * HEAD_DIM:(g + 1) * HEAD_DIM]
            pv = _dot(p.astype(MXU_DTYPE), vaug_scr[g])
            acc_scr[g] = alpha * acc_scr[g] + pv
            m_scr[g] = m_new

    logits(0, sa_scr, ma_scr)
    full_pairs = (nkb - 1) // 2

    def block_pair(ip, carry):
        b0 = 2 * ip
        logits(b0 + 1, sb_scr, mb_scr)
        update(b0, sa_scr, ma_scr)

        @pl.when(b0 + 2 < nkb)
        def _():
            logits(b0 + 2, sa_scr, ma_scr)
            update(b0 + 1, sb_scr, mb_scr)
        return carry

    lax.fori_loop(0, full_pairs, block_pair, 0)

    first_left = 2 * full_pairs
    two_left = first_left + 2 == nkb

    @pl.when(two_left)
    def _():
        logits(first_left + 1, sb_scr, mb_scr)

    @pl.when(first_left < nkb)
    def _():
        update(first_left, sa_scr, ma_scr)

    @pl.when(two_left)
    def _():
        update(first_left + 1, sb_scr, mb_scr)

    for g in range(N_KV_HEADS):
        o = acc_scr[g, :, :HEAD_DIM] / acc_scr[g, :, HEAD_DIM:HEAD_DIM + 1]
        for j in range(GROUP):
            h = g * GROUP + j
            o_ref[:, h * HEAD_DIM:(h + 1) * HEAD_DIM] = o[j * qb:(j + 1) * qb, :]


def _dsa_attention(q, qi, wi, ki_all, k_all, v_all, *, qb, kb, s_true, pos0):
    bsz, t, _ = q.shape
    s_pad = k_all.shape[1]
    nk = s_pad // kb
    n_sel = min(TOPK_KEYS, s_true // 4)
    kit = ki_all.reshape(bsz, nk, kb, IDX_DIM).transpose(0, 1, 3, 2)
    kt = k_all.reshape(bsz, nk, kb, N_KV_HEADS * HEAD_DIM).transpose(0, 1, 3, 2)
    vv = v_all.reshape(bsz, nk, kb, N_KV_HEADS * HEAD_DIM)
    qrow = lambda b, i: (b, i, 0)
    kmap = lambda b, i: (b, 0, 0, 0)
    kern = functools.partial(_dsa_kernel, qb=qb, kb=kb, s_true=s_true, pos0=pos0, n_sel=n_sel)
    return pl.pallas_call(
        kern,
        grid=(bsz, t // qb),
        in_specs=[pl.BlockSpec((None, qb, q.shape[2]), qrow),
                  pl.BlockSpec((None, qb, qi.shape[2]), qrow),
                  pl.BlockSpec((None, qb, wi.shape[2]), qrow),
                  _resident((None, nk, IDX_DIM, kb), kmap),
                  _resident((None, nk, N_KV_HEADS * HEAD_DIM, kb), kmap),
                  _resident((None, nk, kb, N_KV_HEADS * HEAD_DIM), kmap)],
        out_specs=pl.BlockSpec((None, qb, q.shape[2]), qrow),
        out_shape=jax.ShapeDtypeStruct(q.shape, f32),
        scratch_shapes=[pltpu.VMEM((nk, qb, kb), i32),
                        pltpu.VMEM((N_KV_HEADS, HEAD_DIM + qb, kb), MXU_DTYPE),
                        pltpu.VMEM((N_KV_HEADS, GROUP * qb, kb), f32),
                        pltpu.VMEM((N_KV_HEADS, GROUP * qb, kb), f32),
                        pltpu.VMEM((N_KV_HEADS, GROUP * qb, 1), f32),
                        pltpu.VMEM((N_KV_HEADS, GROUP * qb, 1), f32),
                        pltpu.VMEM((N_KV_HEADS, kb, 2 * HEAD_DIM), MXU_DTYPE),
                        pltpu.VMEM((N_KV_HEADS, GROUP * qb, 1), f32),
                        pltpu.VMEM((N_KV_HEADS, GROUP * qb, 2 * HEAD_DIM), f32),
                        pltpu.VMEM((kb, kb), MXU_DTYPE)],
        compiler_params=_cparams(("parallel", "arbitrary")),
        name="dsa_attention",
    )(q, qi, wi, kit, kt, vv)


def _expm1(y):
    u = jnp.exp(y)
    um1 = u - 1.0
    return jnp.where(um1 == 0.0, y, jnp.where(um1 == -1.0, -1.0, um1 * y / jnp.log(u)))


def _rglru_kernel(xr_ref, gr_ref, cs_ref, h0_ref, wc_ref, bc_ref, wax_ref, ba_ref, bx_ref, lam_ref,
                  rnn_ref, hlast_ref, xp_scr, a_scr, b_scr, h_scr, *, reset_first):
    t = pl.program_id(1)
    tt, c = xr_ref.shape
    lead = SUBLANES

    @pl.when(t == 0)
    def _():
        xp_scr[lead - (CONV_W - 1):lead, :] = cs_ref[...]
        h_scr[...] = h0_ref[...]

    xp_scr[lead:lead + tt, :] = xr_ref[...]
    conv = bc_ref[...]
    for j in range(CONV_W):
        off = lead - (CONV_W - 1) + j
        conv = conv + xp_scr[off:off + tt, :] * wc_ref[j:j + 1, :]
    xp_scr[lead - (CONV_W - 1):lead, :] = xp_scr[lead + tt - (CONV_W - 1):lead + tt, :]

    cb = conv.astype(MXU_DTYPE)
    sp = lam_ref[...]
    grow = t * tt + lax.broadcasted_iota(i32, (tt, 1), 0)
    for g in range(c // RNN_GROUP):
        sl = slice(g * RNN_GROUP, (g + 1) * RNN_GROUP)
        z = _dot(cb[:, sl], wax_ref[g])
        r = jax.nn.sigmoid(z[:, :RNN_GROUP] + ba_ref[:, sl])
        ig = jax.nn.sigmoid(z[:, RNN_GROUP:] + bx_ref[:, sl])
        log_a = -LRU_C * r * sp[:, sl]
        mult = jnp.sqrt(-_expm1(2.0 * log_a))
        if reset_first:
            mult = jnp.where(grow == 0, 1.0, mult)
        a_scr[:, sl] = jnp.exp(log_a)
        b_scr[:, sl] = mult * (ig * conv[:, sl])

    def rows(jb, h):
        base = pl.multiple_of(jb * SUBLANES, SUBLANES)
        for u in range(SUBLANES):
            h = a_scr[pl.ds(base + u, 1), :] * h + b_scr[pl.ds(base + u, 1), :]
            rnn_ref[pl.ds(base + u, 1), :] = h
        return h

    h = lax.fori_loop(0, tt // SUBLANES, rows, h_scr[...])
    h_scr[...] = h
    rnn_ref[...] = rnn_ref[...] * jax.nn.gelu(gr_ref[...])

    @pl.when(t == pl.num_programs(1) - 1)
    def _():
        hlast_ref[...] = h


def _blockdiag_groups(w):
    per = RNN_GROUP // RNN_BLOCK
    g = w.reshape(N_RNN_BLOCKS // per, per, RNN_BLOCK, RNN_BLOCK)
    eye = jnp.eye(per, dtype=w.dtype)
    return jnp.einsum('gacd,ab->gacbd', g, eye).reshape(N_RNN_BLOCKS // per, RNN_GROUP, RNN_GROUP)


def _rglru(xr, gr, conv_state, h0, w_conv, b_conv, w_a, b_a, w_x, b_x, lam, *, tt, reset_first):
    bsz, t, c = xr.shape
    wax = jnp.concatenate([_blockdiag_groups(w_a), _blockdiag_groups(w_x)], axis=2).astype(MXU_DTYPE)
    sp = jax.nn.softplus(-lam.astype(f32)).reshape(1, c)
    seq = lambda b, i: (b, i, 0)
    per_b = lambda b, i: (b, 0, 0)
    const2 = lambda b, i: (0, 0)
    const3 = lambda b, i: (0, 0, 0)
    kern = functools.partial(_rglru_kernel, reset_first=reset_first)
    return pl.pallas_call(
        kern,
        grid=(bsz, t // tt),
        in_specs=[pl.BlockSpec((None, tt, c), seq), pl.BlockSpec((None, tt, c), seq),
                  pl.BlockSpec((None, CONV_W - 1, c), per_b), pl.BlockSpec((None, 1, c), per_b),
                  pl.BlockSpec((CONV_W, c), const2), pl.BlockSpec((1, c), const2),
                  pl.BlockSpec(wax.shape, const3), pl.BlockSpec((1, c), const2),
                  pl.BlockSpec((1, c), const2), pl.BlockSpec((1, c), const2)],
        out_specs=[pl.BlockSpec((None, tt, c), seq), pl.BlockSpec((None, 1, c), per_b)],
        out_shape=[jax.ShapeDtypeStruct((bsz, t, c), f32), jax.ShapeDtypeStruct((bsz, 1, c), f32)],
        scratch_shapes=[pltpu.VMEM((tt + SUBLANES, c), f32), pltpu.VMEM((tt, c), f32),
                        pltpu.VMEM((tt, c), f32), pltpu.VMEM((1, c), f32)],
        compiler_params=_cparams(("parallel", "arbitrary")),
        name="rglru",
    )(xr, gr, conv_state, h0.reshape(bsz, 1, c), w_conv, b_conv.reshape(1, c), wax,
      b_a.reshape(1, c), b_x.reshape(1, c), sp)


def _layer_norm(z, g, b):
    mu = jnp.mean(z, axis=-1, keepdims=True)
    var = jnp.mean(jnp.square(z - mu), axis=-1, keepdims=True)
    return (z - mu) * lax.rsqrt(var + LN_EPS) * g + b


def _merge_kernel(attn_ref, rnn_ref, ga_ref, gb_ref, x_ref, wo_ref, g1_ref, b1_ref, wr_ref, br_ref,
                  x1_ref, x1b_ref, cw_ref, *, dn_alpha):
    merged = jax.nn.sigmoid(ga_ref[...]) * attn_ref[...] + jax.nn.sigmoid(gb_ref[...]) * rnn_ref[...]
    m = _dot(merged.astype(MXU_DTYPE), wo_ref[...])
    x1 = _layer_norm(dn_alpha * x_ref[...] + m, g1_ref[...], b1_ref[...])
    x1_ref[...] = x1
    x1b = x1.astype(MXU_DTYPE)
    x1b_ref[...] = x1b
    logits = _dot(x1b, wr_ref[...]) + br_ref[...]
    ne = logits.shape[1]
    eidx = lax.broadcasted_iota(i32, logits.shape, 1).astype(f32)
    work = logits
    chosen = jnp.zeros(logits.shape, jnp.bool_)
    top = None
    for kk in range(TOP_K):
        mx = jnp.max(work, axis=1, keepdims=True)
        if kk == 0:
            top = mx
        first = jnp.min(jnp.where(work == mx, eidx, ne), axis=1, keepdims=True)
        pick = eidx == first
        chosen = chosen | pick
        work = jnp.where(pick, NEG_INF, work)
    e = jnp.where(chosen, jnp.exp(logits - top), 0.0)
    cw_ref[...] = e / jnp.sum(e, axis=1, keepdims=True)


def _merge_ln_router(attn, rnn, ga, gb, x2d, w_out, ln_g, ln_b, w_router, b_router, *, tm, dn_alpha):
    n, d = x2d.shape
    ne = w_router.shape[1]
    row = lambda i: (i, 0)
    const = lambda i: (0, 0)
    kern = functools.partial(_merge_kernel, dn_alpha=dn_alpha)
    return pl.pallas_call(
        kern,
        grid=(n // tm,),
        in_specs=[pl.BlockSpec((tm, d), row)] * 5
                 + [pl.BlockSpec((d, d), const), pl.BlockSpec((1, d), const), pl.BlockSpec((1, d), const),
                    pl.BlockSpec((d, ne), const), pl.BlockSpec((1, ne), const)],
        out_specs=[pl.BlockSpec((tm, d), row), pl.BlockSpec((tm, d), row), pl.BlockSpec((tm, ne), row)],
        out_shape=[jax.ShapeDtypeStruct((n, d), f32), jax.ShapeDtypeStruct((n, d), MXU_DTYPE),
                   jax.ShapeDtypeStruct((n, ne), f32)],
        compiler_params=_cparams(("parallel",)),
        name="merge_ln_router",
    )(attn, rnn, ga, gb, x2d, w_out.astype(MXU_DTYPE), ln_g.reshape(1, d), ln_b.reshape(1, d),
      w_router.astype(MXU_DTYPE), b_router.reshape(1, ne))


MOE_CHUNK = 128
MOE_SUBTILE = 1024


def _moe_kernel(xb_ref, cwt_ref, wg_ref, bg_ref, wu_ref, bu_ref, wd_ref, bd_ref, o_ref, rankt_scr):
    e = pl.program_id(1)
    tm = MOE_SUBTILE if cwt_ref.shape[1] % MOE_SUBTILE == 0 else cwt_ref.shape[1]

    @pl.when(e == 0)
    def _():
        upper = jnp.where(lax.broadcasted_iota(i32, (tm, tm), 0) < lax.broadcasted_iota(i32, (tm, tm), 1),
                          1.0, 0.0).astype(MXU_DTYPE)
        for s0 in range(0, cwt_ref.shape[1], tm):
            selt = jnp.where(cwt_ref[:, s0:s0 + tm] > 0.0, 1.0, 0.0)
            rankt = _dot(selt.astype(MXU_DTYPE), upper)
            rankt_scr[:, s0:s0 + tm] = jnp.where(selt > 0.0, rankt, -1.0)
        o_ref[...] = jnp.zeros(o_ref.shape, f32)

    for s0 in range(0, cwt_ref.shape[1], tm):
        _moe_subtile(e, s0, tm, xb_ref, cwt_ref, wg_ref, bg_ref, wu_ref, bu_ref, wd_ref, bd_ref, o_ref, rankt_scr)


def _moe_subtile(e, s0, tm, xb_ref, cwt_ref, wg_ref, bg_ref, wu_ref, bu_ref, wd_ref, bd_ref, o_ref, rankt_scr):
    rrow = rankt_scr[pl.ds(e, 1), s0:s0 + tm]
    grow = cwt_ref[pl.ds(e, 1), s0:s0 + tm]
    cnt = jnp.max(rrow).astype(i32) + 1

    def run_chunk(first_rank, size):
        lo = first_rank.astype(f32)
        hit = rrow - lo == lax.broadcasted_iota(i32, (size, tm), 0).astype(f32)
        gather = jnp.where(hit, 1.0, 0.0).astype(MXU_DTYPE)
        gcomp = jnp.sum(jnp.where(hit, grow, 0.0), axis=1, keepdims=True)
        xg = _dot(gather, xb_ref[s0:s0 + tm, :]).astype(MXU_DTYPE)
        g = _dot(xg, wg_ref[...]) + bg_ref[...]
        u = _dot(xg, wu_ref[...]) + bu_ref[...]
        g = jnp.minimum(g, SWIGLU_LIMIT)
        u = jnp.clip(u, -SWIGLU_LIMIT, SWIGLU_LIMIT)
        hmid = g * jax.nn.sigmoid(SWIGLU_ALPHA * g) * (u + 1.0)
        y = (_dot(hmid.astype(MXU_DTYPE), wd_ref[...]) + bd_ref[...]) * gcomp
        y_hi = y.astype(MXU_DTYPE)
        y_lo = (y - y_hi.astype(f32)).astype(MXU_DTYPE)
        o_ref[s0:s0 + tm, :] += lax.dot_general(jnp.concatenate([gather, gather], axis=0),
                                                jnp.concatenate([y_hi, y_lo], axis=0),
                                                (((0,), (0,)), ((), ())), preferred_element_type=f32)

    def chunk(cidx, carry):
        run_chunk(cidx * MOE_CHUNK, MOE_CHUNK)
        return carry

    lax.fori_loop(0, (cnt + MOE_CHUNK - 1) // MOE_CHUNK, chunk, 0)


def _moe(x1b, cw, w_gate, b_gate, w_up, b_up, w_down, b_down, *, tm):
    n, d = x1b.shape
    ne = cw.shape[1]
    dff = w_gate.shape[2]
    row = lambda i, e: (i, 0)
    wmap = lambda i, e: (e, 0, 0)
    return pl.pallas_call(
        _moe_kernel,
        grid=(n // tm, ne),
        in_specs=[pl.BlockSpec((tm, d), row), pl.BlockSpec((ne, tm), lambda i, e: (0, i)),
                  pl.BlockSpec((None, d, dff), wmap), pl.BlockSpec((None, 1, dff), wmap),
                  pl.BlockSpec((None, d, dff), wmap), pl.BlockSpec((None, 1, dff), wmap),
                  pl.BlockSpec((None, dff, d), wmap), pl.BlockSpec((None, 1, d), wmap)],
        out_specs=pl.BlockSpec((tm, d), row),
        out_shape=jax.ShapeDtypeStruct((n, d), f32),
        scratch_shapes=[pltpu.VMEM((ne, tm), f32)],
        compiler_params=_cparams(("parallel", "arbitrary")),
        name="moe_experts",
    )(x1b, cw.T, w_gate.astype(MXU_DTYPE), b_gate.reshape(ne, 1, dff), w_up.astype(MXU_DTYPE),
      b_up.reshape(ne, 1, dff), w_down.astype(MXU_DTYPE), b_down.reshape(ne, 1, d))


def _final_kernel(x1_ref, ffn_ref, p_ref, g2_ref, b2_ref, wpg_ref, wpp_ref, y_ref, *, dn_alpha):
    x2 = _layer_norm(dn_alpha * x1_ref[...] + ffn_ref[...], g2_ref[...], b2_ref[...])
    gate = jax.nn.sigmoid(_dot(x2.astype(MXU_DTYPE), wpg_ref[...]))
    emb = _dot(p_ref[...].astype(MXU_DTYPE), wpp_ref[...])
    y_ref[...] = x2 + gate * emb


def _final(x1, ffn, p2d, ln_g, ln_b, w_ple_gate, w_ple_proj, *, tm, dn_alpha):
    n, d = x1.shape
    dp = p2d.shape[1]
    row = lambda i: (i, 0)
    const = lambda i: (0, 0)
    kern = functools.partial(_final_kernel, dn_alpha=dn_alpha)
    return pl.pallas_call(
        kern,
        grid=(n // tm,),
        in_specs=[pl.BlockSpec((tm, d), row), pl.BlockSpec((tm, d), row), pl.BlockSpec((tm, dp), row),
                  pl.BlockSpec((1, d), const), pl.BlockSpec((1, d), const),
                  pl.BlockSpec((d, d), const), pl.BlockSpec((dp, d), const)],
        out_specs=pl.BlockSpec((tm, d), row),
        out_shape=jax.ShapeDtypeStruct((n, d), f32),
        compiler_params=_cparams(("parallel",)),
        name="ln2_ple",
    )(x1, ffn, p2d, ln_g.reshape(1, d), ln_b.reshape(1, d), w_ple_gate.astype(MXU_DTYPE),
      w_ple_proj.astype(MXU_DTYPE))


ROW_TILE = 512
TIME_TILE = 512
MOE_TILE = 2048
KEY_BLOCK = 1024
KEY_BLOCK_SMALL = 3 * LANES


def _pick_tile(n, pref):
    t = min(pref, n)
    while n % t:
        t //= 2
    return t


def _layer(x, p, pos0, past_k, past_v, past_ki, conv_state, h0, prm, depth):
    (w_in, w_conv, b_conv, w_a, b_a, w_x, b_x, lam, w_out, ln1_g, ln1_b, w_router, b_router,
     w_gate, b_gate, w_up, b_up, w_down, b_down, ln2_g, ln2_b, w_ple_gate, w_ple_proj) = prm
    bsz, t, d = x.shape
    n = bsz * t
    past = past_k.shape[1]
    dn_alpha = (2 * depth) ** 0.25

    tm = _pick_tile(n, ROW_TILE)
    pos = pos0 + jnp.arange(t)
    tabs_a = _rope_tables(pos, ROPE_DIM, HEAD_DIM)
    tabs_i = _rope_tables(pos, IDX_ROPE_DIM, IDX_DIM)
    if t % tm:
        rep = tm // t
        tabs_a = tuple(jnp.tile(a, (rep, 1)) for a in tabs_a)
        tabs_i = tuple(jnp.tile(a, (rep, 1)) for a in tabs_i)
    (q, k, kb, v, vb, qi, ki, kib, wi, xr, gr, ga, gb) = _in_projection(
        x.reshape(n, d), _pack_w_in(w_in), tabs_a, tabs_i, tm)

    s_true = past + t
    qb = min(Q_BLOCK, t)
    kblk = KEY_BLOCK if s_true % KEY_BLOCK == 0 else KEY_BLOCK_SMALL
    s_pad = -(-s_true // kblk) * kblk

    def keys_by_position(past_arr, new_arr):
        new_arr = new_arr.reshape(bsz, t, -1)
        parts = [new_arr]
        if past:
            parts.insert(0, past_arr.reshape(bsz, past, new_arr.shape[2]).astype(MXU_DTYPE))
        if s_pad > s_true:
            parts.append(jnp.zeros((bsz, s_pad - s_true, new_arr.shape[2]), MXU_DTYPE))
        return jnp.concatenate(parts, axis=1)

    attn = _dsa_attention(q.reshape(bsz, t, -1), qi.reshape(bsz, t, -1), wi.reshape(bsz, t, -1),
                          keys_by_position(past_ki, kib), keys_by_position(past_k, kb),
                          keys_by_position(past_v, vb), qb=qb, kb=kblk, s_true=s_true, pos0=pos0)

    tt = _pick_tile(t, TIME_TILE)
    xr3 = xr.reshape(bsz, t, -1)
    rnn, h_last = _rglru(xr3, gr.reshape(bsz, t, -1), conv_state, h0, w_conv, b_conv, w_a, b_a, w_x, b_x,
                         lam, tt=tt, reset_first=(pos0 == 0))
    new_conv = jnp.concatenate([conv_state, xr3], axis=1)[:, -(CONV_W - 1):]

    x1, x1b, cw = _merge_ln_router(attn.reshape(n, d), rnn.reshape(n, d), ga, gb, x.reshape(n, d), w_out,
                                   ln1_g, ln1_b, w_router, b_router, tm=tm, dn_alpha=dn_alpha)
    ffn = _moe(x1b, cw, w_gate, b_gate, w_up, b_up, w_down, b_down, tm=_pick_tile(n, MOE_TILE))
    y = _final(x1, ffn, p.reshape(n, -1), ln2_g, ln2_b, w_ple_gate, w_ple_proj, tm=tm, dn_alpha=dn_alpha)

    return (y.reshape(bsz, t, d), k.reshape(bsz, t, N_KV_HEADS, HEAD_DIM), v.reshape(bsz, t, N_KV_HEADS, HEAD_DIM),
            ki.reshape(bsz, t, IDX_DIM), new_conv, h_last.reshape(bsz, -1))


def kernel(x_prompt, x_sample, cache_k, cache_v, cache_kidx, state_conv, state_h, p_prompt, p_sample,
           w_in, w_conv, b_conv, w_a, b_a, w_x, b_x, lru_lambda, w_out, ln1_g, ln1_b, w_router, b_router,
           w_gate, b_gate, w_up, b_up, w_down, b_down, ln2_g, ln2_b, w_ple_gate, w_ple_proj):
    depth = w_in.shape[0]
    bp = x_prompt.shape[0]
    past_len = cache_k.shape[2]
    dt = x_prompt.dtype
    empty_kv = jnp.zeros((bp, 0, N_KV_HEADS, HEAD_DIM), dt)
    empty_ki = jnp.zeros((bp, 0, IDX_DIM), dt)
    zero_conv = jnp.zeros((bp, CONV_W - 1, x_prompt.shape[2]), dt)
    zero_h = jnp.zeros((bp, x_prompt.shape[2]), dt)

    yp, ys = x_prompt, x_sample
    outs_p, outs_s = [], []
    for l in range(depth):
        prm = (w_in[l], w_conv[l], b_conv[l], w_a[l], b_a[l], w_x[l], b_x[l], lru_lambda[l], w_out[l],
               ln1_g[l], ln1_b[l], w_router[l], b_router[l], w_gate[l], b_gate[l], w_up[l], b_up[l],
               w_down[l], b_down[l], ln2_g[l], ln2_b[l], w_ple_gate[l], w_ple_proj[l])
        yp, *rest_p = _layer(yp, p_prompt[l], 0, empty_kv, empty_kv, empty_ki, zero_conv, zero_h, prm, depth)
        ys, *rest_s = _layer(ys, p_sample[l], past_len, cache_k[l], cache_v[l], cache_kidx[l],
                             state_conv[l], state_h[l], prm, depth)
        outs_p.append(rest_p)
        outs_s.append(rest_s)

    stack = lambda outs, j: jnp.stack([o[j] for o in outs])
    return (yp, ys) + tuple(stack(outs_p, j) for j in range(5)) + tuple(stack(outs_s, j) for j in range(5))
```
